```python
import math
import jax, jax.numpy as jnp
from jax import lax
import numpy as np

D_MODEL = 1024
BATCH = 32
SEQ = 2048
DEPTH = 2
DEC_BATCH = 8
DEC_SEQ = 8192
PAST_LEN = 128

GRID_W = 64
EPS = 1e-6
MIX_WIDTH = D_MODEL
FOUR_GROUPS = 4
FOUR_DIM = 64
FOUR_WIDTH = FOUR_GROUPS * FOUR_DIM
DN_HEADS = 4
DN_HEAD_DIM = 64
DN_WIDTH = DN_HEADS * DN_HEAD_DIM
DN_CHUNK = 64
CONV_W = 3
N_HEADS = 8
N_KV_HEADS = 2
HEAD_DIM = 64
GROUP = N_HEADS // N_KV_HEADS
ATTN_WIDTH = N_HEADS * HEAD_DIM
KV_WIDTH = N_KV_HEADS * HEAD_DIM
Q_BLOCK = 128
ROPE_THETA = 10000.0
AXIS_DIM = HEAD_DIM // 2
ROPE_FREQS = AXIS_DIM // 2
MEM_TOKENS = 256
MEM_HEADS = 4
MEM_HEAD_DIM = D_MODEL // MEM_HEADS
D_FF = 2816
IN_SPLITS = (FOUR_WIDTH, 3 * DN_WIDTH, 2 * DN_HEADS, 2 * DN_HEADS, DN_WIDTH, ATTN_WIDTH, KV_WIDTH, KV_WIDTH)
IN_WIDTH = FOUR_WIDTH + 4 * DN_WIDTH + 4 * DN_HEADS + ATTN_WIDTH + 2 * KV_WIDTH

kernel_name = 'hybrid_fourier_deltanet_gqa_encoder'


def rmsnorm(x, w):
    xf = x.astype(jnp.float32)
    y = xf * lax.rsqrt(jnp.mean(xf * xf, axis=-1, keepdims=True) + EPS)
    return (y * w.astype(jnp.float32)).astype(x.dtype)


def l2norm(x):
    return x * lax.rsqrt(jnp.sum(x * x, axis=-1, keepdims=True) + EPS)


def split_cols(z, sizes):
    out, start = [], 0
    for s in sizes:
        out.append(z[..., start:start + s])
        start += s
    return out


def swiglu(x, w_gu, w_down):
    g, u = jnp.split(x @ w_gu, 2, axis=-1)
    return (jax.nn.silu(g) * u) @ w_down


def fourier_mix(u, w_four):
    B, S, _ = u.shape
    ug = u.reshape(B, S, FOUR_GROUPS, FOUR_DIM).astype(jnp.float32)
    f = jnp.fft.fft2(ug, axes=(1, 3), norm='ortho').real
    y = jnp.einsum('bsgc,gcd->bsgd', f.astype(u.dtype), w_four)
    return y.reshape(B, S, FOUR_WIDTH)


def short_conv(z, w):
    pad = CONV_W // 2
    S = z.shape[1]
    zp = jnp.pad(z, ((0, 0), (pad, pad), (0, 0)))
    out = zp[:, 0:S] * w[0]
    for i in range(1, CONV_W):
        out = out + zp[:, i:i + S] * w[i]
    return jax.nn.silu(out)


def gated_delta_chunked(q, k, v, g, beta):
    B, S, H, D = q.shape
    C = DN_CHUNK
    N = S // C

    def to_chunks(t):
        t = t.reshape((B, N, C, H) + t.shape[3:])
        return jnp.moveaxis(t, 3, 1)

    q, k, v = to_chunks(q), to_chunks(k), to_chunks(v)
    g, beta = to_chunks(g), to_chunks(beta)
    gc = jnp.cumsum(g, axis=-1)
    causal = jnp.tril(jnp.ones((C, C), dtype=bool))
    strict = jnp.tril(jnp.ones((C, C), dtype=bool), -1)
    decay = jnp.exp(jnp.where(causal, gc[..., :, None] - gc[..., None, :], -jnp.inf))
    k_beta = k * beta[..., None]
    kk = jnp.einsum('bhnid,bhnjd->bhnij', k_beta, k)
    A = jnp.where(strict, kk * decay, 0.0) + jnp.eye(C, dtype=q.dtype)
    rhs = jnp.concatenate([v * beta[..., None], k_beta * jnp.exp(gc)[..., None]], axis=-1)
    sol = lax.linalg.triangular_solve(A, rhs, left_side=True, lower=True, unit_diagonal=True)
    u_val, w_key = sol[..., :D], sol[..., D:]
    qk = jnp.where(causal, jnp.einsum('bhnid,bhnjd->bhnij', q, k) * decay, 0.0)
    g_last = gc[..., -1]
    k_tail = k * jnp.exp(g_last[..., None] - gc)[..., None]
    q_dec = q * jnp.exp(gc)[..., None]

    def step(state, xs):
        qd, qkc, uc, wc, kt, gl = xs
        v_new = uc - jnp.einsum('bhcd,bhde->bhce', wc, state)
        o = jnp.einsum('bhcd,bhde->bhce', qd, state) + jnp.einsum('bhij,bhje->bhie', qkc, v_new)
        state = state * jnp.exp(gl)[..., None, None] + jnp.einsum('bhcd,bhce->bhde', kt, v_new)
        return state, o

    xs = tuple(jnp.moveaxis(t, 2, 0) for t in (q_dec, qk, u_val, w_key, k_tail, g_last))
    state0 = jnp.zeros((B, H, D, D), q.dtype)
    _, o = lax.scan(step, state0, xs)
    o = jnp.moveaxis(o, 0, 2)
    return jnp.moveaxis(o, 1, 3).reshape(B, S, H, D)


def deltanet_mix(qkv, a, b, gate, conv_w, A_log, dt_bias, out_norm):
    B, S, _ = qkv.shape
    dt = qkv.dtype
    f32 = jnp.float32
    qkv = short_conv(qkv, conv_w).astype(f32)
    q, k, v = jnp.split(qkv, 3, axis=-1)
    q = l2norm(q.reshape(B, S, DN_HEADS, DN_HEAD_DIM)) * (DN_HEAD_DIM ** -0.5)
    k = l2norm(k.reshape(B, S, DN_HEADS, DN_HEAD_DIM))
    v = v.reshape(B, S, DN_HEADS, DN_HEAD_DIM)
    a = a.astype(f32).reshape(B, S, 2, DN_HEADS)
    b = b.astype(f32).reshape(B, S, 2, DN_HEADS)
    g = -jnp.exp(A_log.astype(f32)) * jax.nn.softplus(a + dt_bias.astype(f32))
    beta = jax.nn.sigmoid(b)
    o_f = gated_delta_chunked(q, k, v, g[:, :, 0], beta[:, :, 0])
    flip = lambda t: jnp.flip(t, axis=1)
    o_b = flip(gated_delta_chunked(flip(q), flip(k), flip(v), flip(g[:, :, 1]), flip(beta[:, :, 1])))
    o = rmsnorm(o_f + o_b, out_norm) * jax.nn.silu(gate.astype(f32).reshape(B, S, DN_HEADS, DN_HEAD_DIM))
    return o.reshape(B, S, DN_WIDTH).astype(dt)


def axial_rope(n_tokens):
    rows = n_tokens // GRID_W
    row = jnp.repeat(jnp.arange(rows), GRID_W).astype(jnp.float32)
    col = jnp.tile(jnp.arange(GRID_W), rows).astype(jnp.float32)
    inv = 1.0 / (ROPE_THETA ** (jnp.arange(ROPE_FREQS, dtype=jnp.float32) * (2.0 / AXIS_DIM)))
    ang = jnp.stack([row[:, None] * inv, col[:, None] * inv], axis=1)
    return jnp.cos(ang), jnp.sin(ang)


def apply_rope(x, cos, sin):
    B, S, H, _ = x.shape
    xf = x.astype(jnp.float32).reshape(B, S, H, 2, 2, ROPE_FREQS)
    x1, x2 = xf[..., 0, :], xf[..., 1, :]
    c, s = cos[:, None], sin[:, None]
    out = jnp.stack([x1 * c - x2 * s, x2 * c + x1 * s], axis=-2)
    return out.reshape(B, S, H, HEAD_DIM).astype(x.dtype)


def gqa_attention(q, k, v, q_norm, k_norm, cos, sin):
    B, S, _ = q.shape
    q = apply_rope(rmsnorm(q.reshape(B, S, N_HEADS, HEAD_DIM), q_norm), cos, sin)
    k = apply_rope(rmsnorm(k.reshape(B, S, N_KV_HEADS, HEAD_DIM), k_norm), cos, sin)
    v = v.reshape(B, S, N_KV_HEADS, HEAD_DIM)
    nb = S // Q_BLOCK
    qb = jnp.moveaxis(q.reshape(B, nb, Q_BLOCK, N_KV_HEADS, GROUP, HEAD_DIM), 1, 0)
    scale = HEAD_DIM ** -0.5

    def block(qi):
        s = jnp.einsum('bqkgd,bskd->bkgqs', qi, k).astype(jnp.float32) * scale
        p = jax.nn.softmax(s, axis=-1)
        return jnp.einsum('bkgqs,bskd->bqkgd', p.astype(v.dtype), v)

    o = lax.map(block, qb)
    return jnp.moveaxis(o, 0, 1).reshape(B, S, ATTN_WIDTH)


def memory_cross_attention(h, mem, wq, wkv, wo):
    B, S, _ = h.shape
    M = mem.shape[1]
    q = (h @ wq).reshape(B, S, MEM_HEADS, MEM_HEAD_DIM)
    k, v = jnp.split(mem @ wkv, 2, axis=-1)
    k = k.reshape(B, M, MEM_HEADS, MEM_HEAD_DIM)
    v = v.reshape(B, M, MEM_HEADS, MEM_HEAD_DIM)
    s = jnp.einsum('bshd,bmhd->bhsm', q, k).astype(jnp.float32) * (MEM_HEAD_DIM ** -0.5)
    p = jax.nn.softmax(s, axis=-1)
    o = jnp.einsum('bhsm,bmhd->bshd', p.astype(v.dtype), v).reshape(B, S, D_MODEL)
    return o @ wo


def encoder(x, mem, weights):
    (ffn1_norm, ffn1_w_gu, ffn1_w_down, mix_norm, w_in, four_w, dn_conv, dn_A_log, dn_dt_bias,
     dn_out_norm, attn_q_norm, attn_k_norm, w_out, mem_norm_x, mem_norm_m, mem_wq, mem_wkv, mem_wo,
     ffn2_norm, ffn2_w_gu, ffn2_w_down, final_norm) = weights
    S = x.shape[1]
    cos, sin = axial_rope(S)
    for l in range(DEPTH):
        x = x + 0.5 * swiglu(rmsnorm(x, ffn1_norm[l]), ffn1_w_gu[l], ffn1_w_down[l])
        n = rmsnorm(x, mix_norm[l])
        z = n @ w_in[l]
        u_f, dn_qkv, dn_a, dn_b, dn_gate, a_q, a_k, a_v = split_cols(z, IN_SPLITS)
        y_f = fourier_mix(u_f, four_w[l])
        y_d = deltanet_mix(dn_qkv, dn_a, dn_b, dn_gate, dn_conv[l], dn_A_log[l], dn_dt_bias[l], dn_out_norm[l])
        y_a = gqa_attention(a_q, a_k, a_v, attn_q_norm[l], attn_k_norm[l], cos, sin)
        x = x + jnp.concatenate([y_f, y_d, y_a], axis=-1) @ w_out[l]
        x = x + memory_cross_attention(rmsnorm(x, mem_norm_x[l]), rmsnorm(mem, mem_norm_m[l]),
                                       mem_wq[l], mem_wkv[l], mem_wo[l])
        x = x + 0.5 * swiglu(rmsnorm(x, ffn2_norm[l]), ffn2_w_gu[l], ffn2_w_down[l])
    return rmsnorm(x, final_norm)


def setup_inputs(seed: int = 0) -> dict:
    key = jax.random.key(seed)
    ks = jax.random.split(key, 32)
    f32 = jnp.float32
    L = DEPTH

    def dense(k, shape, fan_in):
        return jax.random.normal(k, shape, f32) * (fan_in ** -0.5)

    def gain(k, shape):
        return 1.0 + 0.02 * jax.random.normal(k, shape, f32)

    dt = jnp.exp(jax.random.uniform(ks[10], (L, 2, DN_HEADS), f32, math.log(1e-3), math.log(1e-1)))
    return {
        'x_prompt': jax.random.normal(ks[0], (BATCH, SEQ, D_MODEL), f32),
        'x_sample': jax.random.normal(ks[1], (DEC_BATCH, DEC_SEQ, D_MODEL), f32),
        'mem_prompt': jax.random.normal(ks[2], (BATCH, MEM_TOKENS, D_MODEL), f32),
        'mem_sample': jax.random.normal(ks[3], (DEC_BATCH, MEM_TOKENS, D_MODEL), f32),
        'ffn1_norm': gain(ks[4], (L, D_MODEL)),
        'ffn1_w_gu': dense(ks[5], (L, D_MODEL, 2 * D_FF), D_MODEL),
        'ffn1_w_down': dense(ks[6], (L, D_FF, D_MODEL), D_FF),
        'mix_norm': gain(ks[7], (L, D_MODEL)),
        'w_in': dense(ks[8], (L, D_MODEL, IN_WIDTH), D_MODEL),
        'four_w': dense(ks[9], (L, FOUR_GROUPS, FOUR_DIM, FOUR_DIM), FOUR_DIM),
        'dn_conv': dense(ks[11], (L, CONV_W, 3 * DN_WIDTH), CONV_W),
        'dn_A_log': jnp.log(jax.random.uniform(ks[12], (L, 2, DN_HEADS), f32, 1.0, 16.0)),
        'dn_dt_bias': dt + jnp.log(-jnp.expm1(-dt)),
        'dn_out_norm': gain(ks[13], (L, DN_HEAD_DIM)),
        'attn_q_norm': gain(ks[14], (L, HEAD_DIM)),
        'attn_k_norm': gain(ks[15], (L, HEAD_DIM)),
        'w_out': dense(ks[16], (L, MIX_WIDTH, D_MODEL), MIX_WIDTH),
        'mem_norm_x': gain(ks[17], (L, D_MODEL)),
        'mem_norm_m': gain(ks[18], (L, D_MODEL)),
        'mem_wq': dense(ks[19], (L, D_MODEL, D_MODEL), D_MODEL),
        'mem_wkv': dense(ks[20], (L, D_MODEL, 2 * D_MODEL), D_MODEL),
        'mem_wo': dense(ks[21], (L, D_MODEL, D_MODEL), D_MODEL),
        'ffn2_norm': gain(ks[22], (L, D_MODEL)),
        'ffn2_w_gu': dense(ks[23], (L, D_MODEL, 2 * D_FF), D_MODEL),
        'ffn2_w_down': dense(ks[24], (L, D_FF, D_MODEL), D_FF),
        'final_norm': gain(ks[25], (D_MODEL,)),
    }


def reference(x_prompt, x_sample, mem_prompt, mem_sample, ffn1_norm, ffn1_w_gu, ffn1_w_down, mix_norm,
              w_in, four_w, dn_conv, dn_A_log, dn_dt_bias, dn_out_norm, attn_q_norm, attn_k_norm, w_out,
              mem_norm_x, mem_norm_m, mem_wq, mem_wkv, mem_wo, ffn2_norm, ffn2_w_gu, ffn2_w_down, final_norm):
    weights = (ffn1_norm, ffn1_w_gu, ffn1_w_down, mix_norm, w_in, four_w, dn_conv, dn_A_log, dn_dt_bias,
               dn_out_norm, attn_q_norm, attn_k_norm, w_out, mem_norm_x, mem_norm_m, mem_wq, mem_wkv, mem_wo,
               ffn2_norm, ffn2_w_gu, ffn2_w_down, final_norm)
    y_prompt = encoder(x_prompt, mem_prompt, weights)
    y_sample = encoder(x_sample, mem_sample, weights)
    return (y_prompt, y_sample)
```

```python
import functools
import math

import jax
import jax.numpy as jnp
from jax import lax
from jax.experimental import pallas as pl
from jax.experimental.pallas import tpu as pltpu

F32 = jnp.float32
BF16 = jnp.bfloat16

D_MODEL = 1024
DEPTH = 2
GRID_W = 64
EPS = 1e-6
FOUR_GROUPS = 4
FOUR_DIM = 64
FOUR_WIDTH = FOUR_GROUPS * FOUR_DIM
DN_HEADS = 4
DN_HEAD_DIM = 64
DN_WIDTH = DN_HEADS * DN_HEAD_DIM
DN_CHUNK = 64
CONV_W = 3
N_HEADS = 8
N_KV_HEADS = 2
HEAD_DIM = 64
GROUP = N_HEADS // N_KV_HEADS
ATTN_WIDTH = N_HEADS * HEAD_DIM
KV_WIDTH = N_KV_HEADS * HEAD_DIM
ROPE_THETA = 10000.0
AXIS_DIM = HEAD_DIM // 2
ROPE_FREQS = AXIS_DIM // 2
MEM_TOKENS = 256
MEM_HEADS = 4
MEM_HEAD_DIM = D_MODEL // MEM_HEADS
D_FF = 2816

V7X_VMEM_LIMIT_BYTES = 56 * 1024 * 1024
LANES = 128

TOKEN_TILE = 512
FF_CHUNK = 256
ATT_TQ = 256
ATT_TK = 512
FOUR_LANE_TILE = 2048
FOUR_T = 8
IN_MAIN = FOUR_WIDTH + 3 * DN_WIDTH + DN_WIDTH
IN_ATT = ATTN_WIDTH + 2 * KV_WIDTH
IN_PERM = IN_MAIN + IN_ATT + LANES


def _cparams(*sem):
    return pltpu.CompilerParams(dimension_semantics=sem, vmem_limit_bytes=V7X_VMEM_LIMIT_BYTES)


def _resident(shape):
    nd = len(shape)
    return pl.BlockSpec(shape, lambda *_: (0,) * nd, pipeline_mode=pl.Buffered(1))


def _dot(a, b, dims):
    return lax.dot_general(a, b, (dims, ((), ())), preferred_element_type=F32)


def _mm(a, b):
    return _dot(a, b, ((1,), (0,)))


def _split3(x):
    hi = x.astype(BF16)
    r = x - hi.astype(F32)
    mid = r.astype(BF16)
    lo = (r - mid.astype(F32)).astype(BF16)
    return hi, mid, lo


def _mm_exact_rhs(x, sel):
    hi, mid, lo = _split3(x)
    return _mm(hi, sel) + _mm(mid, sel) + _mm(lo, sel)


def _mm_exact_lhs(sel, x):
    hi, mid, lo = _split3(x)
    return _mm(sel, hi) + _mm(sel, mid) + _mm(sel, lo)


def _rms(x, w):
    ms = jnp.mean(x * x, axis=-1, keepdims=True)
    return x * lax.rsqrt(ms + EPS) * w


def _group_mean_sq(x, ones_bd):
    return _mm_exact_rhs(x * x, ones_bd) * (1.0 / 64.0)


def _ffn_core(x, nw_ref, wgu_ref, wd_ref, h_scr):
    n = _rms(x, nw_ref[...]).astype(BF16)
    for c in range(D_FF // FF_CHUNK):
        lo = c * FF_CHUNK
        g = _mm(n, wgu_ref[:, lo:lo + FF_CHUNK])
        u = _mm(n, wgu_ref[:, D_FF + lo:D_FF + lo + FF_CHUNK])
        h_scr[:, lo:lo + FF_CHUNK] = (g * jax.nn.sigmoid(g) * u).astype(BF16)
    return x + 0.5 * _mm(h_scr[...], wd_ref[...])


def _ffn_kernel(x_ref, nw_ref, wgu_ref, wd_ref, fw_ref, o_ref, h_scr, *, final_norm):
    y = _ffn_core(x_ref[...], nw_ref, wgu_ref, wd_ref, h_scr)
    if final_norm:
        y = _rms(y, fw_ref[...])
    o_ref[...] = y


def _ffn(x, nw, wgu, wd, fw, final_norm):
    t = x.shape[0]
    tm = TOKEN_TILE
    row = pl.BlockSpec((tm, D_MODEL), lambda i: (i, 0))
    return pl.pallas_call(
        functools.partial(_ffn_kernel, final_norm=final_norm),
        grid=(t // tm,),
        in_specs=[row, _resident((1, D_MODEL)), _resident((D_MODEL, 2 * D_FF)), _resident((D_FF, D_MODEL)),
                  _resident((1, D_MODEL))],
        out_specs=row,
        out_shape=jax.ShapeDtypeStruct((t, D_MODEL), F32),
        scratch_shapes=[pltpu.VMEM((tm, D_FF), BF16)],
        compiler_params=_cparams("parallel"),
        name="ffn",
    )(x, nw, wgu, wd, fw)


def _ffn_mix_kernel(x_ref, n1_ref, wgu_ref, wd_ref, nm_ref, win_ref, qw_ref, kw_ref, cos_ref, sin_ref, ones_ref,
                    xo_ref, uf_ref, qkv_ref, gate_ref, aq_ref, akt_ref, av_ref, ab_ref, h_scr):
    x1 = _ffn_core(x_ref[...], n1_ref, wgu_ref, wd_ref, h_scr)
    xo_ref[...] = x1
    n = _rms(x1, nm_ref[...]).astype(BF16)
    z1 = _mm(n, win_ref[:, 0:IN_MAIN])
    uf_ref[...] = z1[:, 0:FOUR_WIDTH].astype(BF16)
    qkv_ref[...] = z1[:, FOUR_WIDTH:FOUR_WIDTH + 3 * DN_WIDTH].astype(BF16)
    gate_ref[...] = z1[:, FOUR_WIDTH + 3 * DN_WIDTH:IN_MAIN].astype(BF16)

    z2 = _mm(n, win_ref[:, IN_MAIN:IN_MAIN + IN_ATT])
    av_ref[...] = z2[:, ATTN_WIDTH + KV_WIDTH:IN_ATT].astype(BF16)
    cos = cos_ref[...]
    sin = sin_ref[...]
    ones_bd = ones_ref[...]
    lane = lax.broadcasted_iota(jnp.int32, cos.shape, 1)
    first_half = (lane & (AXIS_DIM - 1)) < ROPE_FREQS

    def norm_rope(xb, w):
        y = xb * lax.rsqrt(_group_mean_sq(xb, ones_bd) + EPS) * w
        partner = jnp.where(first_half, pltpu.roll(y, LANES - ROPE_FREQS, 1), pltpu.roll(y, ROPE_FREQS, 1))
        return y * cos + partner * sin

    qw = qw_ref[...]
    for j in range(ATTN_WIDTH // LANES):
        qj = norm_rope(z2[:, j * LANES:(j + 1) * LANES], qw) * (HEAD_DIM ** -0.5)
        aq_ref[:, j * LANES:(j + 1) * LANES] = qj.astype(BF16)
    kk = norm_rope(z2[:, ATTN_WIDTH:ATTN_WIDTH + KV_WIDTH], kw_ref[...])
    akt_ref[...] = kk.T.astype(BF16)

    z3 = _mm(n, win_ref[:, IN_MAIN + IN_ATT:IN_PERM])
    ab_ref[...] = z3[:, 0:4 * DN_HEADS]


def _ffn_mix(x, seq, n1, wgu, wd, nm, win, qw, kw, cos, sin, ones128):
    t = x.shape[0]
    tm = TOKEN_TILE
    nb = t // seq
    per_seq = seq // tm

    def row(width):
        return pl.BlockSpec((tm, width), lambda i: (i, 0))

    tab = pl.BlockSpec((tm, LANES), lambda i: (i % per_seq, 0))
    outs = (
        jax.ShapeDtypeStruct((t, D_MODEL), F32),
        jax.ShapeDtypeStruct((t, FOUR_WIDTH), BF16),
        jax.ShapeDtypeStruct((t, 3 * DN_WIDTH), BF16),
        jax.ShapeDtypeStruct((t, DN_WIDTH), BF16),
        jax.ShapeDtypeStruct((t, ATTN_WIDTH), BF16),
        jax.ShapeDtypeStruct((nb, KV_WIDTH, seq), BF16),
        jax.ShapeDtypeStruct((t, KV_WIDTH), BF16),
        jax.ShapeDtypeStruct((t, 4 * DN_HEADS), F32),
    )
    out_specs = (
        row(D_MODEL), row(FOUR_WIDTH), row(3 * DN_WIDTH), row(DN_WIDTH), row(ATTN_WIDTH),
        pl.BlockSpec((None, KV_WIDTH, tm), lambda i: (i // per_seq, 0, i % per_seq)),
        row(KV_WIDTH), row(4 * DN_HEADS),
    )
    return pl.pallas_call(
        _ffn_mix_kernel,
        grid=(t // tm,),
        in_specs=[row(D_MODEL), _resident((1, D_MODEL)), _resident((D_MODEL, 2 * D_FF)), _resident((D_FF, D_MODEL)),
                  _resident((1, D_MODEL)), _resident((D_MODEL, IN_PERM)), _resident((1, LANES)), _resident((1, LANES)),
                  tab, tab, _resident((LANES, LANES))],
        out_specs=out_specs,
        out_shape=outs,
        scratch_shapes=[pltpu.VMEM((tm, D_FF), BF16)],
        compiler_params=_cparams("parallel"),
        name="ffn_mix",
    )(x, n1, wgu, wd, nm, win, qw, kw, cos, sin, ones128)


def _four_rows_kernel(x_ref, cs_ref, re_ref, im_ref, *, rows):
    y = _mm(cs_ref[...], x_ref[...])
    re_ref[...] = y[0:rows].astype(BF16)
    im_ref[...] = y[rows:2 * rows].astype(BF16)


def _four_rows(u, cs):
    nb, rows, width = u.shape
    tn = FOUR_LANE_TILE
    blk = pl.BlockSpec((None, rows, tn), lambda b, j: (b, 0, j))
    out = jax.ShapeDtypeStruct((nb, rows, width), BF16)
    return pl.pallas_call(
        functools.partial(_four_rows_kernel, rows=rows),
        grid=(nb, width // tn),
        in_specs=[blk, _resident((2 * rows, rows))],
        out_specs=(blk, blk),
        out_shape=(out, out),
        compiler_params=_cparams("parallel", "parallel"),
        name="four_rows",
    )(u, cs)


def _four_cols_kernel(re_ref, im_ref, tw_ref, ch_ref, w_ref, o_ref, ore_scr, oim_scr, *, scale):
    for t in range(FOUR_T):
        rhs = jnp.concatenate([re_ref[t], im_ref[t]], axis=0)
        o = _mm(tw_ref[t], rhs)
        ore_scr[t * GRID_W:(t + 1) * GRID_W, :] = o[0:GRID_W].astype(BF16)
        oim_scr[t * GRID_W:(t + 1) * GRID_W, :] = o[GRID_W:2 * GRID_W].astype(BF16)
    f = _mm(ore_scr[...], ch_ref[0:FOUR_WIDTH, :]) + _mm(oim_scr[...], ch_ref[FOUR_WIDTH:2 * FOUR_WIDTH, :])
    y = _mm((f * scale).astype(BF16), w_ref[...])
    o_ref[...] = y.reshape(FOUR_T, GRID_W, FOUR_WIDTH).astype(BF16)


def _four_cols(yre, yim, tw, ch, wbd, scale):
    nb, rows = yre.shape[0], yre.shape[1]
    blk = pl.BlockSpec((None, FOUR_T, GRID_W, FOUR_WIDTH), lambda b, i: (b, i, 0, 0))
    return pl.pallas_call(
        functools.partial(_four_cols_kernel, scale=scale),
        grid=(nb, rows // FOUR_T),
        in_specs=[blk, blk, pl.BlockSpec((FOUR_T, 2 * GRID_W, 2 * GRID_W), lambda b, i: (i, 0, 0)),
                  _resident((2 * FOUR_WIDTH, FOUR_WIDTH)), _resident((FOUR_WIDTH, FOUR_WIDTH))],
        out_specs=blk,
        out_shape=jax.ShapeDtypeStruct((nb, rows, GRID_W, FOUR_WIDTH), BF16),
        scratch_shapes=[pltpu.VMEM((FOUR_T * GRID_W, FOUR_WIDTH), BF16)] * 2,
        compiler_params=_cparams("parallel", "parallel"),
        name="four_cols",
    )(yre, yim, tw, ch, wbd)


def _fourier_tables(seq):
    rows = seq // GRID_W
    two_pi = 2.0 * math.pi
    kr = jnp.arange(rows, dtype=jnp.int32)
    ang_r = ((kr[:, None] * kr[None, :]) % rows).astype(F32) * (two_pi / rows)
    cs = jnp.concatenate([jnp.cos(ang_r), -jnp.sin(ang_r)], axis=0).astype(BF16)
    c = jnp.arange(GRID_W, dtype=jnp.int32)
    k = rows * c[None, :, None] + kr[:, None, None]
    ang = ((k * c[None, None, :]) % seq).astype(F32) * (two_pi / seq)
    cg, sg = jnp.cos(ang), jnp.sin(ang)
    tw = jnp.concatenate([jnp.concatenate([cg, sg], axis=2), jnp.concatenate([-sg, cg], axis=2)], axis=1).astype(BF16)
    ang_c = ((c[:, None] * c[None, :]) % FOUR_DIM).astype(F32) * (two_pi / FOUR_DIM)
    eye = jnp.eye(FOUR_GROUPS, dtype=F32)
    ch = jnp.concatenate([jnp.kron(eye, jnp.cos(ang_c)), jnp.kron(eye, jnp.sin(ang_c))], axis=0).astype(BF16)
    return cs, tw, ch


DN_HALO = 16


def _dn_prep_kernel(x_ref, prev_ref, next_ref, ab_ref, cw_ref, na_ref, dtb_ref, ones_ref,
                    q_ref, k_ref, v_ref, gb_ref, *, tiles_per_seq):
    i = pl.program_id(1)
    x = x_ref[...].astype(F32)
    ts = x.shape[0]
    has_prev = jnp.where(i > 0, 1.0, 0.0)
    has_next = jnp.where(i < tiles_per_seq - 1, 1.0, 0.0)
    prev_row = prev_ref[...].astype(F32)[DN_HALO - 1:DN_HALO, :] * has_prev
    next_row = next_ref[...].astype(F32)[0:1, :] * has_next
    r = lax.broadcasted_iota(jnp.int32, x.shape, 0)
    xm = jnp.where(r == 0, prev_row, pltpu.roll(x, 1, 0))
    xp = jnp.where(r == ts - 1, next_row, pltpu.roll(x, ts - 1, 0))
    cw = cw_ref[...]
    y = xm * cw[0:1, :] + x * cw[1:2, :] + xp * cw[2:3, :]
    y = y * jax.nn.sigmoid(y)
    ones_bd = ones_ref[...]

    def l2n(z):
        return z * lax.rsqrt(_group_mean_sq(z, ones_bd) * float(DN_HEAD_DIM) + EPS)

    q_ref[...] = (l2n(y[:, 0:DN_WIDTH]) * (DN_HEAD_DIM ** -0.5)).astype(BF16)
    k_ref[...] = l2n(y[:, DN_WIDTH:2 * DN_WIDTH]).astype(BF16)
    v_ref[...] = y[:, 2 * DN_WIDTH:3 * DN_WIDTH].astype(BF16)

    t = ab_ref[...] + dtb_ref[...]
    softplus = jnp.maximum(t, 0.0) + jnp.log1p(jnp.exp(-jnp.abs(t)))
    lane = lax.broadcasted_iota(jnp.int32, t.shape, 1)
    gb_ref[...] = jnp.where(lane < 2 * DN_HEADS, -jnp.exp(na_ref[...]) * softplus, jax.nn.sigmoid(t))


def _dn_prep(qkv, ab, cw, na, dtb, ones256):
    nb, seq, _ = qkv.shape
    ts = TOKEN_TILE
    tiles = seq // ts
    hpt = ts // DN_HALO
    n_halo = seq // DN_HALO

    def blk(width):
        return pl.BlockSpec((None, ts, width), lambda b, i: (b, i, 0))

    out = jax.ShapeDtypeStruct((nb, seq, DN_WIDTH), BF16)
    return pl.pallas_call(
        functools.partial(_dn_prep_kernel, tiles_per_seq=tiles),
        grid=(nb, tiles),
        in_specs=[blk(3 * DN_WIDTH),
                  pl.BlockSpec((None, DN_HALO, 3 * DN_WIDTH), lambda b, i: (b, jnp.maximum(i * hpt - 1, 0), 0)),
                  pl.BlockSpec((None, DN_HALO, 3 * DN_WIDTH), lambda b, i: (b, jnp.minimum((i + 1) * hpt, n_halo - 1), 0)),
                  blk(4 * DN_HEADS), _resident((CONV_W, 3 * DN_WIDTH)), _resident((1, 4 * DN_HEADS)),
                  _resident((1, 4 * DN_HEADS)), _resident((DN_WIDTH, DN_WIDTH))],
        out_specs=(blk(DN_WIDTH), blk(DN_WIDTH), blk(DN_WIDTH), blk(4 * DN_HEADS)),
        out_shape=(out, out, out, jax.ShapeDtypeStruct((nb, seq, 4 * DN_HEADS), F32)),
        compiler_params=_cparams("parallel", "parallel"),
        name="dn_prep",
    )(qkv, qkv, qkv, ab, cw, na, dtb, ones256)


def _dn_kernel(q_ref, k_ref, v_ref, gb_ref, gate_ref, onw_ref, ones_ref, eg_ref, eb_ref, o_ref, acc_scr, s_scr,
               *, n_chunks):
    C = DN_CHUNK
    W = DN_WIDTH
    row = lax.broadcasted_iota(jnp.int32, (C, W), 0)
    col = lax.broadcasted_iota(jnp.int32, (C, W), 1) & (C - 1)
    eye = (row == col).astype(F32)
    bd = (lax.broadcasted_iota(jnp.int32, (W, W), 0) >> 6) == (lax.broadcasted_iota(jnp.int32, (W, W), 1) >> 6)
    tri_r = lax.broadcasted_iota(jnp.int32, (C, C), 0)
    tri_c = lax.broadcasted_iota(jnp.int32, (C, C), 1)
    acc_scr[...] = jnp.zeros_like(acc_scr)
    s_scr[...] = jnp.zeros_like(s_scr)

    def bdiag(y):
        yb = y.astype(BF16)
        t = jnp.concatenate([yb, yb, yb, yb], axis=0)
        return jnp.where(bd, t, jnp.zeros_like(t))

    def prod(x, ybd):
        return _mm(x.astype(BF16), ybd)

    def unit(d, c):
        r0 = pl.multiple_of(c * C, C)
        q = q_ref[pl.ds(r0, C), :].astype(F32)
        k = k_ref[pl.ds(r0, C), :].astype(F32)
        v = v_ref[pl.ds(r0, C), :].astype(F32)
        gb = gb_ref[pl.ds(r0, C), :]
        if d == 0:
            incl, strict, tri = row >= col, row > col, tri_r >= tri_c
        else:
            incl, strict, tri = row <= col, row < col, tri_r <= tri_c
        gcum = _mm_exact_lhs(tri.astype(BF16), gb)
        gc = _mm_exact_rhs(gcum, eg_ref[d])
        beta = _mm_exact_rhs(gb, eb_ref[d])
        gct = jnp.sum(gc * eye, axis=0, keepdims=True)
        g_last = gc[C - 1:C, :] if d == 0 else gc[0:1, :]
        decay = jnp.exp(jnp.where(incl, gc - gct, -1e30))
        egc = jnp.exp(gc)
        kb = k * beta
        kbd_t = bdiag(k)
        kk = _dot(kb.astype(BF16), kbd_t, ((1,), (1,)))
        qk = _dot(q.astype(BF16), kbd_t, ((1,), (1,)))
        low = jnp.where(strict, kk * decay, 0.0)
        qkm = jnp.where(incl, qk * decay, 0.0)
        inv = eye - low
        lbd = bdiag(low)
        power = low
        for _ in range(5):
            power = prod(power, lbd)
            lbd = bdiag(power)
            inv = inv + prod(inv, lbd)
        u = prod(inv, bdiag(v * beta))
        w = prod(inv, bdiag(kb * egc))
        state = s_scr[d]
        ws = _mm(jnp.concatenate([w, q * egc], axis=0).astype(BF16), state.astype(BF16))
        v_new = u - ws[0:C]
        o = ws[C:2 * C] + prod(qkm, bdiag(v_new))
        k_tail = k * jnp.exp(g_last - gc)
        upd = _dot(k_tail.astype(BF16), v_new.astype(BF16), ((0,), (0,)))
        s_scr[d] = state * jnp.exp(g_last) + jnp.where(bd, upd, 0.0)
        acc_scr[pl.ds(r0, C), :] += o

    def body(it, carry):
        unit(0, it)
        unit(1, n_chunks - 1 - it)
        return carry

    lax.fori_loop(0, n_chunks, body, 0)

    ones_bd = ones_ref[...]
    onw = onw_ref[...]
    ft = TOKEN_TILE

    def fin(t, carry):
        r0 = pl.multiple_of(t * ft, ft)
        o = acc_scr[pl.ds(r0, ft), :]
        g = gate_ref[pl.ds(r0, ft), :].astype(F32)
        y = o * lax.rsqrt(_group_mean_sq(o, ones_bd) + EPS) * onw
        o_ref[pl.ds(r0, ft), :] = (y * (g * jax.nn.sigmoid(g))).astype(BF16)
        return carry

    lax.fori_loop(0, (n_chunks * C) // ft, fin, 0)


def _dn(q, k, v, gb, gate, onw, ones256, eg, eb):
    nb, seq, _ = q.shape

    def blk(width, **kw):
        return pl.BlockSpec((None, seq, width), lambda b: (b, 0, 0), **kw)

    one = dict(pipeline_mode=pl.Buffered(1))
    return pl.pallas_call(
        functools.partial(_dn_kernel, n_chunks=seq // DN_CHUNK),
        grid=(nb,),
        in_specs=[blk(DN_WIDTH, **one), blk(DN_WIDTH, **one), blk(DN_WIDTH, **one), blk(4 * DN_HEADS, **one),
                  blk(DN_WIDTH, **one),
                  _resident((1, DN_WIDTH)), _resident((DN_WIDTH, DN_WIDTH)),
                  _resident((2, 4 * DN_HEADS, DN_WIDTH)), _resident((2, 4 * DN_HEADS, DN_WIDTH))],
        out_specs=blk(DN_WIDTH),
        out_shape=jax.ShapeDtypeStruct((nb, seq, DN_WIDTH), BF16),
        scratch_shapes=[pltpu.VMEM((seq, DN_WIDTH), F32), pltpu.VMEM((2, DN_WIDTH, DN_WIDTH), F32)],
        compiler_params=_cparams("parallel"),
        name="deltanet",
    )(q, k, v, gb, gate, onw, ones256, eg, eb)


def _attn_kernel(q_ref, kt_ref, v_ref, o_ref, qs_scr, m_scr, acc_scr):
    tq = q_ref.shape[0]
    tk = v_ref.shape[0]
    kj = pl.program_id(2)
    gq = GROUP * tq

    @pl.when(kj == 0)
    def _():
        lane = lax.broadcasted_iota(jnp.int32, (tq, LANES), 1)
        for h in range(N_HEADS):
            blk = q_ref[:, (h // 2) * LANES:(h // 2 + 1) * LANES]
            keep = (lane < HEAD_DIM) if h % 2 == 0 else (lane >= HEAD_DIM)
            qs_scr[h * tq:(h + 1) * tq, :] = jnp.where(keep, blk, jnp.zeros_like(blk))
        m_scr[...] = jnp.full_like(m_scr, -jnp.inf)
        acc_scr[...] = jnp.zeros_like(acc_scr)

    vblk = v_ref[...]
    lane_v = lax.broadcasted_iota(jnp.int32, (tk, LANES), 1)
    for kv in range(N_KV_HEADS):
        rows = slice(kv * gq, (kv + 1) * gq)
        kt = kt_ref[kv * HEAD_DIM:(kv + 1) * HEAD_DIM, :]
        s = _mm(qs_scr[rows, :], jnp.concatenate([kt, kt], axis=0))
        m_old = m_scr[rows, :]
        m_new = jnp.maximum(m_old, jnp.max(s, axis=1, keepdims=True))
        alpha = jnp.exp(m_old - m_new)
        p = jnp.exp(s - m_new).astype(BF16)
        own = (lane_v < HEAD_DIM) if kv == 0 else (lane_v >= HEAD_DIM)
        vx = jnp.where(own, vblk, jnp.ones_like(vblk))
        acc_scr[rows, :] = alpha * acc_scr[rows, :] + _mm(p, vx)
        m_scr[rows, :] = m_new

    @pl.when(kj == pl.num_programs(2) - 1)
    def _():
        lane = lax.broadcasted_iota(jnp.int32, (tq, LANES), 1)
        for kv in range(N_KV_HEADS):
            a = acc_scr[kv * gq:(kv + 1) * gq, :]
            o = a / pltpu.roll(a, HEAD_DIM, 1)
            for j in range(GROUP // 2):
                even = o[(2 * j) * tq:(2 * j + 1) * tq]
                odd = o[(2 * j + 1) * tq:(2 * j + 2) * tq]
                if kv == 0:
                    tile = jnp.where(lane < HEAD_DIM, even, pltpu.roll(odd, HEAD_DIM, 1))
                else:
                    tile = jnp.where(lane < HEAD_DIM, pltpu.roll(even, HEAD_DIM, 1), odd)
                c0 = kv * GROUP * HEAD_DIM + j * LANES
                o_ref[:, c0:c0 + LANES] = tile.astype(BF16)


def _attention(q, kt, v):
    nb, seq, _ = q.shape
    tq, tk = ATT_TQ, ATT_TK
    return pl.pallas_call(
        _attn_kernel,
        grid=(nb, seq // tq, seq // tk),
        in_specs=[pl.BlockSpec((None, tq, ATTN_WIDTH), lambda b, i, j: (b, i, 0)),
                  pl.BlockSpec((None, KV_WIDTH, tk), lambda b, i, j: (b, 0, j)),
                  pl.BlockSpec((None, tk, KV_WIDTH), lambda b, i, j: (b, j, 0))],
        out_specs=pl.BlockSpec((None, tq, ATTN_WIDTH), lambda b, i, j: (b, i, 0)),
        out_shape=jax.ShapeDtypeStruct((nb, seq, ATTN_WIDTH), BF16),
        scratch_shapes=[pltpu.VMEM((N_HEADS * tq, LANES), BF16), pltpu.VMEM((N_HEADS * tq, 1), F32),
                        pltpu.VMEM((N_HEADS * tq, LANES), F32)],
        compiler_params=_cparams("parallel", "parallel", "arbitrary"),
        name="gqa_attention",
    )(q, kt, v)


def _mem_kv_kernel(m_ref, nw_ref, wkv_ref, kt_ref, v_ref):
    mn = _rms(m_ref[...], nw_ref[...]).astype(BF16)
    k = _mm(mn, wkv_ref[:, 0:D_MODEL])
    kt_ref[...] = k.T.astype(BF16)
    v_ref[...] = _mm(mn, wkv_ref[:, D_MODEL:2 * D_MODEL]).astype(BF16)


def _mem_kv(mem, nw, wkv):
    nb = mem.shape[0]
    return pl.pallas_call(
        _mem_kv_kernel,
        grid=(nb,),
        in_specs=[pl.BlockSpec((None, MEM_TOKENS, D_MODEL), lambda b: (b, 0, 0)), _resident((1, D_MODEL)),
                  _resident((D_MODEL, 2 * D_MODEL))],
        out_specs=(pl.BlockSpec((None, D_MODEL, MEM_TOKENS), lambda b: (b, 0, 0)),
                   pl.BlockSpec((None, MEM_TOKENS, D_MODEL), lambda b: (b, 0, 0))),
        out_shape=(jax.ShapeDtypeStruct((nb, D_MODEL, MEM_TOKENS), BF16),
                   jax.ShapeDtypeStruct((nb, MEM_TOKENS, D_MODEL), BF16)),
        compiler_params=_cparams("parallel"),
        name="mem_kv",
    )(mem, nw, wkv)


def _out_mem_kernel(x_ref, yf_ref, yd_ref, ya_ref, wout_ref, nx_ref, wq_ref, kt_ref, v_ref, wo_ref, o_ref, o_scr,
                    *, rows):
    blk = yf_ref[...]
    yf = jnp.concatenate([blk[:, j * FOUR_WIDTH:(j + 1) * FOUR_WIDTH] for j in range(blk.shape[1] // FOUR_WIDTH)],
                         axis=0)
    x1 = (x_ref[...] + _mm(yf, wout_ref[0:FOUR_WIDTH, :])
          + _mm(yd_ref[...], wout_ref[FOUR_WIDTH:FOUR_WIDTH + DN_WIDTH, :])
          + _mm(ya_ref[...], wout_ref[FOUR_WIDTH + DN_WIDTH:D_MODEL, :]))
    h = _rms(x1, nx_ref[...]).astype(BF16)
    q = (_mm(h, wq_ref[...]) * (MEM_HEAD_DIM ** -0.5)).astype(BF16)
    for hh in range(MEM_HEADS):
        sl = slice(hh * MEM_HEAD_DIM, (hh + 1) * MEM_HEAD_DIM)
        s = _mm(q[:, sl], kt_ref[sl, :])
        p = jnp.exp(s - jnp.max(s, axis=1, keepdims=True))
        o = _mm(p.astype(BF16), v_ref[:, sl]) / jnp.sum(p, axis=1, keepdims=True)
        o_scr[:, sl] = o.astype(BF16)
    o_ref[...] = x1 + _mm(o_scr[...], wo_ref[...])


def _out_mem(x, seq, yf, yd, ya, wout, nx, wq, kt, v, wo):
    t = x.shape[0]
    tm = TOKEN_TILE
    per_seq = seq // tm
    rows = seq // GRID_W
    jw = (tm // rows) * FOUR_WIDTH

    def row(width):
        return pl.BlockSpec((tm, width), lambda i: (i, 0))

    return pl.pallas_call(
        functools.partial(_out_mem_kernel, rows=rows),
        grid=(t // tm,),
        in_specs=[row(D_MODEL),
                  pl.BlockSpec((None, rows, jw), lambda i: (i // per_seq, 0, i % per_seq)),
                  row(DN_WIDTH), row(ATTN_WIDTH), _resident((D_MODEL, D_MODEL)), _resident((1, D_MODEL)),
                  _resident((D_MODEL, D_MODEL)),
                  pl.BlockSpec((None, D_MODEL, MEM_TOKENS), lambda i: (i // per_seq, 0, 0)),
                  pl.BlockSpec((None, MEM_TOKENS, D_MODEL), lambda i: (i // per_seq, 0, 0)),
                  _resident((D_MODEL, D_MODEL))],
        out_specs=row(D_MODEL),
        out_shape=jax.ShapeDtypeStruct((t, D_MODEL), F32),
        scratch_shapes=[pltpu.VMEM((tm, D_MODEL), BF16)],
        compiler_params=_cparams("parallel"),
        name="out_mem",
    )(x, yf, yd, ya, wout, nx, wq, kt, v, wo)


def _rope_tables(seq):
    rows = seq // GRID_W
    r = jnp.repeat(jnp.arange(rows), GRID_W).astype(F32)
    c = jnp.tile(jnp.arange(GRID_W), rows).astype(F32)
    inv = 1.0 / (ROPE_THETA ** (jnp.arange(ROPE_FREQS, dtype=F32) * (2.0 / AXIS_DIM)))
    ang = jnp.stack([r[:, None] * inv, c[:, None] * inv], axis=1)
    cos = jnp.cos(ang)[:, :, None, :]
    sin = jnp.sin(ang)[:, :, None, :]
    cos = jnp.broadcast_to(cos, (seq, 2, 2, ROPE_FREQS)).reshape(seq, HEAD_DIM)
    sin = jnp.concatenate([-sin, sin], axis=2).reshape(seq, HEAD_DIM)
    return jnp.tile(cos, (1, LANES // HEAD_DIM)), jnp.tile(sin, (1, LANES // HEAD_DIM))


def _ones_blockdiag(n, g):
    i = jnp.arange(n) // g
    return (i[:, None] == i[None, :]).astype(BF16)


def _head_expanders():
    src = jnp.arange(4 * DN_HEADS)[:, None]
    head = (jnp.arange(DN_WIDTH) // DN_HEAD_DIM)[None, :]
    eg = jnp.stack([(src == d * DN_HEADS + head) for d in range(2)]).astype(BF16)
    eb = jnp.stack([(src == 2 * DN_HEADS + d * DN_HEADS + head) for d in range(2)]).astype(BF16)
    return eg, eb


def _encoder(x, mem, lw, final_norm):
    nb, seq, _ = x.shape
    t = nb * seq
    rows = seq // GRID_W
    cos, sin = _rope_tables(seq)
    cs, tw, ch = _fourier_tables(seq)
    ones128 = _ones_blockdiag(LANES, HEAD_DIM)
    ones256 = _ones_blockdiag(DN_WIDTH, DN_HEAD_DIM)
    eg, eb = _head_expanders()
    four_scale = 1.0 / math.sqrt(seq * FOUR_DIM)
    x2 = x.reshape(t, D_MODEL)
    for l in range(DEPTH):
        w = lw[l]
        x2, uf, qkv, gate, aq, akt, av, ab = _ffn_mix(x2, seq, w["n1"], w["wgu1"], w["wd1"], w["nm"], w["win"],
                                                      w["qw"], w["kw"], cos, sin, ones128)
        yre, yim = _four_rows(uf.reshape(nb, rows, GRID_W * FOUR_WIDTH), cs)
        yf = _four_cols(yre.reshape(nb, rows, GRID_W, FOUR_WIDTH), yim.reshape(nb, rows, GRID_W, FOUR_WIDTH),
                        tw, ch, w["fw"], four_scale)
        dq, dk, dv, gb = _dn_prep(qkv.reshape(nb, seq, 3 * DN_WIDTH), ab.reshape(nb, seq, 4 * DN_HEADS),
                                  w["cw"], w["na"], w["dtb"], ones256)
        yd = _dn(dq, dk, dv, gb, gate.reshape(nb, seq, DN_WIDTH), w["onw"], ones256, eg, eb)
        ya = _attention(aq.reshape(nb, seq, ATTN_WIDTH), akt, av.reshape(nb, seq, KV_WIDTH))
        mkt, mv = _mem_kv(mem, w["nmm"], w["wkv"])
        x2 = _out_mem(x2, seq, yf.reshape(nb, rows, GRID_W * FOUR_WIDTH), yd.reshape(t, DN_WIDTH),
                      ya.reshape(t, ATTN_WIDTH), w["wout"], w["nmx"], w["wq"], mkt, mv, w["wo"])
        x2 = _ffn(x2, w["n2"], w["wgu2"], w["wd2"], final_norm, final_norm=(l == DEPTH - 1))
    return x2.reshape(nb, seq, D_MODEL)


def _prep_weights(ffn1_norm, ffn1_w_gu, ffn1_w_down, mix_norm, w_in, four_w, dn_conv, dn_A_log, dn_dt_bias,
                  dn_out_norm, attn_q_norm, attn_k_norm, w_out, mem_norm_x, mem_norm_m, mem_wq, mem_wkv, mem_wo,
                  ffn2_norm, ffn2_w_gu, ffn2_w_down):
    lw = []
    o_ab = FOUR_WIDTH + 3 * DN_WIDTH
    o_gate = o_ab + 4 * DN_HEADS
    for l in range(DEPTH):
        wi = w_in[l]
        win = jnp.concatenate([wi[:, :o_ab], wi[:, o_gate:], wi[:, o_ab:o_gate],
                               jnp.zeros((D_MODEL, LANES - 4 * DN_HEADS), F32)], axis=1).astype(BF16)
        fw = jnp.zeros((FOUR_WIDTH, FOUR_WIDTH), F32)
        for g in range(FOUR_GROUPS):
            fw = lax.dynamic_update_slice(fw, four_w[l, g], (g * FOUR_DIM, g * FOUR_DIM))
        zeros8 = jnp.zeros((2 * DN_HEADS,), F32)
        lw.append(dict(
            n1=ffn1_norm[l].reshape(1, D_MODEL), wgu1=ffn1_w_gu[l].astype(BF16), wd1=ffn1_w_down[l].astype(BF16),
            nm=mix_norm[l].reshape(1, D_MODEL), win=win,
            qw=jnp.tile(attn_q_norm[l], LANES // HEAD_DIM).reshape(1, LANES),
            kw=jnp.tile(attn_k_norm[l], LANES // HEAD_DIM).reshape(1, LANES),
            fw=fw.astype(BF16), cw=dn_conv[l],
            na=jnp.concatenate([dn_A_log[l].reshape(-1), zeros8]).reshape(1, 4 * DN_HEADS),
            dtb=jnp.concatenate([dn_dt_bias[l].reshape(-1), zeros8]).reshape(1, 4 * DN_HEADS),
            onw=jnp.tile(dn_out_norm[l], DN_HEADS).reshape(1, DN_WIDTH),
            wout=w_out[l].astype(BF16), nmx=mem_norm_x[l].reshape(1, D_MODEL), nmm=mem_norm_m[l].reshape(1, D_MODEL),
            wq=mem_wq[l].astype(BF16), wkv=mem_wkv[l].astype(BF16), wo=mem_wo[l].astype(BF16),
            n2=ffn2_norm[l].reshape(1, D_MODEL), wgu2=ffn2_w_gu[l].astype(BF16), wd2=ffn2_w_down[l].astype(BF16),
        ))
    return lw


def kernel(x_prompt, x_sample, mem_prompt, mem_sample, ffn1_norm, ffn1_w_gu, ffn1_w_down, mix_norm, w_in, four_w, dn_conv, dn_A_log, dn_dt_bias, dn_out_norm, attn_q_norm, attn_k_norm, w_out, mem_norm_x, mem_norm_m, mem_wq, mem_wkv, mem_wo, ffn2_norm, ffn2_w_gu, ffn2_w_down, final_norm):
    lw = _prep_weights(ffn1_norm, ffn1_w_gu, ffn1_w_down, mix_norm, w_in, four_w, dn_conv, dn_A_log, dn_dt_bias,
                       dn_out_norm, attn_q_norm, attn_k_norm, w_out, mem_norm_x, mem_norm_m, mem_wq, mem_wkv, mem_wo,
                       ffn2_norm, ffn2_w_gu, ffn2_w_down)
    fn = final_norm.reshape(1, D_MODEL)
    return (_encoder(x_prompt, mem_prompt, lw, fn), _encoder(x_sample, mem_sample, lw, fn))
```

```python
import functools
import math

import jax
import jax.numpy as jnp
from jax import lax
from jax.experimental import pallas as pl
from jax.experimental.pallas import tpu as pltpu

F32 = jnp.float32
BF16 = jnp.bfloat16

D_MODEL = 1024
DEPTH = 2
GRID_W = 64
EPS = 1e-6
FOUR_GROUPS = 4
FOUR_DIM = 64
FOUR_WIDTH = FOUR_GROUPS * FOUR_DIM
DN_HEADS = 4
DN_HEAD_DIM = 64
DN_WIDTH = DN_HEADS * DN_HEAD_DIM
DN_CHUNK = 64
CONV_W = 3
N_HEADS = 8
N_KV_HEADS = 2
HEAD_DIM = 64
GROUP = N_HEADS // N_KV_HEADS
ATTN_WIDTH = N_HEADS * HEAD_DIM
KV_WIDTH = N_KV_HEADS * HEAD_DIM
ROPE_THETA = 10000.0
AXIS_DIM = HEAD_DIM // 2
ROPE_FREQS = AXIS_DIM // 2
MEM_TOKENS = 256
MEM_HEADS = 4
MEM_HEAD_DIM = D_MODEL // MEM_HEADS
D_FF = 2816

V7X_VMEM_LIMIT_BYTES = 56 * 1024 * 1024
LANES = 128

TOKEN_TILE = 512
FF_CHUNK = 256
ATT_TQ = 512
ATT_TK = 512
FOUR_LANE_TILE = 2048
FOUR_T = 8
IN_MAIN = FOUR_WIDTH + 3 * DN_WIDTH + DN_WIDTH
IN_ATT = ATTN_WIDTH + 2 * KV_WIDTH
IN_PERM = IN_MAIN + IN_ATT + LANES
ATT_Q_SCALE = HEAD_DIM ** -0.5 * math.log2(math.e)


def _cparams(*sem):
    return pltpu.CompilerParams(dimension_semantics=sem, vmem_limit_bytes=V7X_VMEM_LIMIT_BYTES)


def _resident(shape):
    nd = len(shape)
    return pl.BlockSpec(shape, lambda *_: (0,) * nd, pipeline_mode=pl.Buffered(1))


def _dot(a, b, dims):
    return lax.dot_general(a, b, (dims, ((), ())), preferred_element_type=F32)


def _mm(a, b):
    return _dot(a, b, ((1,), (0,)))


def _split3(x):
    hi = x.astype(BF16)
    r = x - hi.astype(F32)
    mid = r.astype(BF16)
    lo = (r - mid.astype(F32)).astype(BF16)
    return hi, mid, lo


def _mm_exact_rhs(x, sel):
    hi, mid, lo = _split3(x)
    return _mm(hi, sel) + _mm(mid, sel) + _mm(lo, sel)


def _mm_exact_lhs(sel, x):
    hi, mid, lo = _split3(x)
    return _mm(sel, hi) + _mm(sel, mid) + _mm(sel, lo)


def _rms(x, w):
    ms = jnp.mean(x * x, axis=-1, keepdims=True)
    return x * lax.rsqrt(ms + EPS) * w


def _group_mean_sq(x, ones_bd):
    return _mm_exact_rhs(x * x, ones_bd) * (1.0 / 64.0)


def _ffn_core(x, nw_ref, wgu_ref, wd_ref, h_scr):
    n = _rms(x, nw_ref[...]).astype(BF16)
    for c in range(D_FF // FF_CHUNK):
        lo = c * FF_CHUNK
        g = _mm(n, wgu_ref[:, lo:lo + FF_CHUNK])
        u = _mm(n, wgu_ref[:, D_FF + lo:D_FF + lo + FF_CHUNK])
        h_scr[:, lo:lo + FF_CHUNK] = (g * jax.nn.sigmoid(g) * u).astype(BF16)
    return x + 0.5 * _mm(h_scr[...], wd_ref[...])


def _ffn_kernel(x_ref, nw_ref, wgu_ref, wd_ref, fw_ref, o_ref, h_scr, *, final_norm):
    y = _ffn_core(x_ref[...], nw_ref, wgu_ref, wd_ref, h_scr)
    if final_norm:
        y = _rms(y, fw_ref[...])
    o_ref[...] = y


def _ffn(x, nw, wgu, wd, fw, final_norm):
    t = x.shape[0]
    tm = TOKEN_TILE
    row = pl.BlockSpec((tm, D_MODEL), lambda i: (i, 0))
    return pl.pallas_call(
        functools.partial(_ffn_kernel, final_norm=final_norm),
        grid=(t // tm,),
        in_specs=[row, _resident((1, D_MODEL)), _resident((D_MODEL, 2 * D_FF)), _resident((D_FF, D_MODEL)),
                  _resident((1, D_MODEL))],
        out_specs=row,
        out_shape=jax.ShapeDtypeStruct((t, D_MODEL), F32),
        scratch_shapes=[pltpu.VMEM((tm, D_FF), BF16)],
        compiler_params=_cparams("parallel"),
        name="ffn",
    )(x, nw, wgu, wd, fw)


def _ffn_mix_kernel(x_ref, n1_ref, wgu_ref, wd_ref, nm_ref, win_ref, qw_ref, kw_ref, cos_ref, sin_ref, ones_ref,
                    xo_ref, uf_ref, qkv_ref, gate_ref, aqt_ref, ak_ref, avt_ref, ab_ref, h_scr):
    x1 = _ffn_core(x_ref[...], n1_ref, wgu_ref, wd_ref, h_scr)
    xo_ref[...] = x1
    n = _rms(x1, nm_ref[...]).astype(BF16)
    z1 = _mm(n, win_ref[:, 0:IN_MAIN])
    uf_ref[...] = z1[:, 0:FOUR_WIDTH].astype(BF16)
    qkv_ref[...] = z1[:, FOUR_WIDTH:FOUR_WIDTH + 3 * DN_WIDTH].astype(BF16)
    gate_ref[...] = z1[:, FOUR_WIDTH + 3 * DN_WIDTH:IN_MAIN].astype(BF16)

    z2 = _mm(n, win_ref[:, IN_MAIN:IN_MAIN + IN_ATT])
    avt_ref[...] = z2[:, ATTN_WIDTH + KV_WIDTH:IN_ATT].T.astype(BF16)
    cos = cos_ref[...]
    sin = sin_ref[...]
    ones_bd = ones_ref[...]
    lane = lax.broadcasted_iota(jnp.int32, cos.shape, 1)
    first_half = (lane & (AXIS_DIM - 1)) < ROPE_FREQS

    def norm_rope(xb, w):
        y = xb * lax.rsqrt(_group_mean_sq(xb, ones_bd) + EPS) * w
        partner = jnp.where(first_half, pltpu.roll(y, LANES - ROPE_FREQS, 1), pltpu.roll(y, ROPE_FREQS, 1))
        return y * cos + partner * sin

    qw = qw_ref[...]
    for j in range(ATTN_WIDTH // LANES):
        qj = norm_rope(z2[:, j * LANES:(j + 1) * LANES], qw) * ATT_Q_SCALE
        aqt_ref[j * LANES:(j + 1) * LANES, :] = qj.T.astype(BF16)
    ak_ref[...] = norm_rope(z2[:, ATTN_WIDTH:ATTN_WIDTH + KV_WIDTH], kw_ref[...]).astype(BF16)

    z3 = _mm(n, win_ref[:, IN_MAIN + IN_ATT:IN_PERM])
    ab_ref[...] = z3[:, 0:4 * DN_HEADS]


def _ffn_mix(x, seq, n1, wgu, wd, nm, win, qw, kw, cos, sin, ones128):
    t = x.shape[0]
    tm = TOKEN_TILE
    nb = t // seq
    per_seq = seq // tm

    def row(width):
        return pl.BlockSpec((tm, width), lambda i: (i, 0))

    tab = pl.BlockSpec((tm, LANES), lambda i: (i % per_seq, 0))
    outs = (
        jax.ShapeDtypeStruct((t, D_MODEL), F32),
        jax.ShapeDtypeStruct((t, FOUR_WIDTH), BF16),
        jax.ShapeDtypeStruct((t, 3 * DN_WIDTH), BF16),
        jax.ShapeDtypeStruct((t, DN_WIDTH), BF16),
        jax.ShapeDtypeStruct((nb, ATTN_WIDTH, seq), BF16),
        jax.ShapeDtypeStruct((t, KV_WIDTH), BF16),
        jax.ShapeDtypeStruct((nb, KV_WIDTH, seq), BF16),
        jax.ShapeDtypeStruct((t, 4 * DN_HEADS), F32),
    )

    def col(height):
        return pl.BlockSpec((None, height, tm), lambda i: (i // per_seq, 0, i % per_seq))

    out_specs = (
        row(D_MODEL), row(FOUR_WIDTH), row(3 * DN_WIDTH), row(DN_WIDTH), col(ATTN_WIDTH),
        row(KV_WIDTH), col(KV_WIDTH), row(4 * DN_HEADS),
    )
    return pl.pallas_call(
        _ffn_mix_kernel,
        grid=(t // tm,),
        in_specs=[row(D_MODEL), _resident((1, D_MODEL)), _resident((D_MODEL, 2 * D_FF)), _resident((D_FF, D_MODEL)),
                  _resident((1, D_MODEL)), _resident((D_MODEL, IN_PERM)), _resident((1, LANES)), _resident((1, LANES)),
                  tab, tab, _resident((LANES, LANES))],
        out_specs=out_specs,
        out_shape=outs,
        scratch_shapes=[pltpu.VMEM((tm, D_FF), BF16)],
        compiler_params=_cparams("parallel"),
        name="ffn_mix",
    )(x, n1, wgu, wd, nm, win, qw, kw, cos, sin, ones128)


def _four_rows_kernel(x_ref, cs_ref, re_ref, im_ref, *, rows):
    y = _mm(cs_ref[...], x_ref[...])
    re_ref[...] = y[0:rows].astype(BF16)
    im_ref[...] = y[rows:2 * rows].astype(BF16)


def _four_rows(u, cs):
    nb, rows, width = u.shape
    tn = FOUR_LANE_TILE
    blk = pl.BlockSpec((None, rows, tn), lambda b, j: (b, 0, j))
    out = jax.ShapeDtypeStruct((nb, rows, width), BF16)
    return pl.pallas_call(
        functools.partial(_four_rows_kernel, rows=rows),
        grid=(nb, width // tn),
        in_specs=[blk, _resident((2 * rows, rows))],
        out_specs=(blk, blk),
        out_shape=(out, out),
        compiler_params=_cparams("parallel", "parallel"),
        name="four_rows",
    )(u, cs)


def _four_cols_kernel(re_ref, im_ref, tw_ref, ch_ref, w_ref, o_ref, ore_scr, oim_scr, *, scale):
    for t in range(FOUR_T):
        rhs = jnp.concatenate([re_ref[t], im_ref[t]], axis=0)
        o = _mm(tw_ref[t], rhs)
        ore_scr[t * GRID_W:(t + 1) * GRID_W, :] = o[0:GRID_W].astype(BF16)
        oim_scr[t * GRID_W:(t + 1) * GRID_W, :] = o[GRID_W:2 * GRID_W].astype(BF16)
    f = _mm(ore_scr[...], ch_ref[0:FOUR_WIDTH, :]) + _mm(oim_scr[...], ch_ref[FOUR_WIDTH:2 * FOUR_WIDTH, :])
    y = _mm((f * scale).astype(BF16), w_ref[...])
    o_ref[...] = y.reshape(FOUR_T, GRID_W, FOUR_WIDTH).astype(BF16)


def _four_cols(yre, yim, tw, ch, wbd, scale):
    nb, rows = yre.shape[0], yre.shape[1]
    blk = pl.BlockSpec((None, FOUR_T, GRID_W, FOUR_WIDTH), lambda b, i: (b, i, 0, 0))
    return pl.pallas_call(
        functools.partial(_four_cols_kernel, scale=scale),
        grid=(nb, rows // FOUR_T),
        in_specs=[blk, blk, pl.BlockSpec((FOUR_T, 2 * GRID_W, 2 * GRID_W), lambda b, i: (i, 0, 0)),
                  _resident((2 * FOUR_WIDTH, FOUR_WIDTH)), _resident((FOUR_WIDTH, FOUR_WIDTH))],
        out_specs=blk,
        out_shape=jax.ShapeDtypeStruct((nb, rows, GRID_W, FOUR_WIDTH), BF16),
        scratch_shapes=[pltpu.VMEM((FOUR_T * GRID_W, FOUR_WIDTH), BF16)] * 2,
        compiler_params=_cparams("parallel", "parallel"),
        name="four_cols",
    )(yre, yim, tw, ch, wbd)


def _fourier_tables(seq):
    rows = seq // GRID_W
    two_pi = 2.0 * math.pi
    kr = jnp.arange(rows, dtype=jnp.int32)
    ang_r = ((kr[:, None] * kr[None, :]) % rows).astype(F32) * (two_pi / rows)
    cs = jnp.concatenate([jnp.cos(ang_r), -jnp.sin(ang_r)], axis=0).astype(BF16)
    c = jnp.arange(GRID_W, dtype=jnp.int32)
    k = rows * c[None, :, None] + kr[:, None, None]
    ang = ((k * c[None, None, :]) % seq).astype(F32) * (two_pi / seq)
    cg, sg = jnp.cos(ang), jnp.sin(ang)
    tw = jnp.concatenate([jnp.concatenate([cg, sg], axis=2), jnp.concatenate([-sg, cg], axis=2)], axis=1).astype(BF16)
    ang_c = ((c[:, None] * c[None, :]) % FOUR_DIM).astype(F32) * (two_pi / FOUR_DIM)
    eye = jnp.eye(FOUR_GROUPS, dtype=F32)
    ch = jnp.concatenate([jnp.kron(eye, jnp.cos(ang_c)), jnp.kron(eye, jnp.sin(ang_c))], axis=0).astype(BF16)
    return cs, tw, ch


DN_HALO = 16


def _dn_prep_kernel(x_ref, prev_ref, next_ref, ab_ref, cw_ref, na_ref, dtb_ref, ones_ref,
                    q_ref, k_ref, v_ref, gb_ref, *, tiles_per_seq):
    i = pl.program_id(1)
    x = x_ref[...].astype(F32)
    ts = x.shape[0]
    has_prev = jnp.where(i > 0, 1.0, 0.0)
    has_next = jnp.where(i < tiles_per_seq - 1, 1.0, 0.0)
    prev_row = prev_ref[...].astype(F32)[DN_HALO - 1:DN_HALO, :] * has_prev
    next_row = next_ref[...].astype(F32)[0:1, :] * has_next
    r = lax.broadcasted_iota(jnp.int32, x.shape, 0)
    xm = jnp.where(r == 0, prev_row, pltpu.roll(x, 1, 0))
    xp = jnp.where(r == ts - 1, next_row, pltpu.roll(x, ts - 1, 0))
    cw = cw_ref[...]
    y = xm * cw[0:1, :] + x * cw[1:2, :] + xp * cw[2:3, :]
    y = y * jax.nn.sigmoid(y)
    ones_bd = ones_ref[...]

    def l2n(z):
        return z * lax.rsqrt(_group_mean_sq(z, ones_bd) * float(DN_HEAD_DIM) + EPS)

    q_ref[...] = (l2n(y[:, 0:DN_WIDTH]) * (DN_HEAD_DIM ** -0.5)).astype(BF16)
    k_ref[...] = l2n(y[:, DN_WIDTH:2 * DN_WIDTH]).astype(BF16)
    v_ref[...] = y[:, 2 * DN_WIDTH:3 * DN_WIDTH].astype(BF16)

    t = ab_ref[...] + dtb_ref[...]
    softplus = jnp.maximum(t, 0.0) + jnp.log1p(jnp.exp(-jnp.abs(t)))
    lane = lax.broadcasted_iota(jnp.int32, t.shape, 1)
    gb_ref[...] = jnp.where(lane < 2 * DN_HEADS, -jnp.exp(na_ref[...]) * softplus, jax.nn.sigmoid(t))


def _dn_prep(qkv, ab, cw, na, dtb, ones256):
    nb, seq, _ = qkv.shape
    ts = TOKEN_TILE
    tiles = seq // ts
    hpt = ts // DN_HALO
    n_halo = seq // DN_HALO

    def blk(width):
        return pl.BlockSpec((None, ts, width), lambda b, i: (b, i, 0))

    out = jax.ShapeDtypeStruct((nb, seq, DN_WIDTH), BF16)
    return pl.pallas_call(
        functools.partial(_dn_prep_kernel, tiles_per_seq=tiles),
        grid=(nb, tiles),
        in_specs=[blk(3 * DN_WIDTH),
                  pl.BlockSpec((None, DN_HALO, 3 * DN_WIDTH), lambda b, i: (b, jnp.maximum(i * hpt - 1, 0), 0)),
                  pl.BlockSpec((None, DN_HALO, 3 * DN_WIDTH), lambda b, i: (b, jnp.minimum((i + 1) * hpt, n_halo - 1), 0)),
                  blk(4 * DN_HEADS), _resident((CONV_W, 3 * DN_WIDTH)), _resident((1, 4 * DN_HEADS)),
                  _resident((1, 4 * DN_HEADS)), _resident((DN_WIDTH, DN_WIDTH))],
        out_specs=(blk(DN_WIDTH), blk(DN_WIDTH), blk(DN_WIDTH), blk(4 * DN_HEADS)),
        out_shape=(out, out, out, jax.ShapeDtypeStruct((nb, seq, 4 * DN_HEADS), F32)),
        compiler_params=_cparams("parallel", "parallel"),
        name="dn_prep",
    )(qkv, qkv, qkv, ab, cw, na, dtb, ones256)


DN_NB = 4
DN_CT = 8
DN_BD_BUFS = 7


def _dn_kernel(qf_ref, kf_ref, vf_ref, gf_ref, qb_ref, kb_ref, vb_ref, gbk_ref, mask_ref, tri_ref, eg_ref, eb_ref,
               of_ref, ob_ref, s_scr, bd_scr, *, nbat):
    C = DN_CHUNK
    j = pl.program_id(1)

    @pl.when(j == 0)
    def _():
        s_scr[...] = jnp.zeros_like(s_scr)
        bd_scr[...] = jnp.zeros_like(bd_scr)

    low_half = lax.broadcasted_iota(jnp.int32, (C, LANES), 1) < DN_HEAD_DIM
    nt = (((1,), (1,)))

    def set_bd(slot, buf, y):
        yb = y.astype(BF16)
        for h in range(DN_HEADS):
            t = h // 2
            tile = yb[:, t * LANES:(t + 1) * LANES]
            keep = low_half if h % 2 == 0 else jnp.logical_not(low_half)
            bd_scr[slot, buf, h * C:(h + 1) * C, t * LANES:(t + 1) * LANES] = jnp.where(keep, tile,
                                                                                        jnp.zeros_like(tile))
        return bd_scr[slot, buf]

    def prod(x, ybd):
        return _mm(x.astype(BF16), ybd)

    def st_load(u):
        d, bb, r0 = u["d"], u["bb"], u["r0"]
        q_ref, k_ref, v_ref, g_ref = u["in"]
        q = q_ref[bb, pl.ds(r0, C), :].astype(F32)
        k = k_ref[bb, pl.ds(r0, C), :].astype(F32)
        v = v_ref[bb, pl.ds(r0, C), :].astype(F32)
        gb = g_ref[bb, pl.ds(r0, C), :]
        u.update(q=q, k=k, v=v, beta=_mm_exact_rhs(gb, eb_ref[d]),
                 gcum=_mm_exact_lhs(tri_ref[d], gb))

    def st_decay(u):
        u["gc"] = _mm_exact_rhs(u["gcum"], eg_ref[u["d"]])

    def st_scale(u):
        d, gc, q, k = u["d"], u["gc"], u["q"], u["k"]
        u["gct"] = jnp.sum(gc * mask_ref[d, 2], axis=0, keepdims=True)
        u["g_last"] = gc[C - 1:C, :] if d == 0 else gc[0:1, :]
        kb = k * u["beta"]
        egc = jnp.exp(gc)
        u.update(kb=kb, vbeta=u["v"] * u["beta"], kbe=kb * egc, qd=q * egc)

    def st_gram(u):
        d = u["d"]
        incl, strict, eye = mask_ref[d, 0], mask_ref[d, 1], mask_ref[d, 2]
        r1 = _dot(jnp.concatenate([u["kb"], u["q"], eye], axis=0).astype(BF16), set_bd(u["slot"], 0, u["k"]), nt)
        decay = jnp.exp(jnp.minimum(u["gc"] - u["gct"], 0.0))
        low = r1[0:C] * (decay * strict)
        u["qkm"] = r1[C:2 * C] * (decay * incl)
        u["kt_tail"] = r1[2 * C:3 * C] * jnp.exp(u["g_last"] - u["gct"])
        u["inv"] = eye - low
        u["power"] = low
        u["lbd"] = set_bd(u["slot"], 1, low)

    def st_square(i):
        def f(u):
            u["power"] = prod(u["power"], u["lbd"])
            u["lbd"] = set_bd(u["slot"], 2 - i % 2, u["power"])
        return f

    def st_accum(u):
        u["inv"] = u["inv"] + prod(u["inv"], u["lbd"])

    def st_solve(u):
        u["u"] = prod(u["inv"], set_bd(u["slot"], 3, u["vbeta"]))
        u["w"] = prod(u["inv"], set_bd(u["slot"], 4, u["kbe"]))

    def st_state_in(u):
        state = s_scr[u["slot"]]
        ws = _mm(jnp.concatenate([u["w"], u["qd"]], axis=0).astype(BF16), set_bd(u["slot"], 5, state))
        u["state"] = state
        u["v_new"] = u["u"] - ws[0:C]
        u["o"] = ws[C:2 * C]

    def st_state_out(u):
        r2 = _mm(jnp.concatenate([u["qkm"], u["kt_tail"]], axis=0).astype(BF16), set_bd(u["slot"], 6, u["v_new"]))
        s_scr[u["slot"]] = u["state"] * jnp.exp(u["g_last"]) + r2[C:2 * C]
        u["out"][u["bb"], pl.ds(u["r0"], C), :] = (u["o"] + r2[0:C]).astype(BF16)

    stages = [st_load, st_decay, st_scale, st_gram]
    for i in range(5):
        stages += [st_square(i), st_accum]
    stages += [st_solve, st_state_in, st_state_out]

    def body(c, carry):
        units = []
        for bb in range(nbat):
            units.append(dict(d=0, bb=bb, slot=2 * bb, r0=pl.multiple_of(c * C, C),
                              **{"in": (qf_ref, kf_ref, vf_ref, gf_ref), "out": of_ref}))
            units.append(dict(d=1, bb=bb, slot=2 * bb + 1, r0=pl.multiple_of((DN_CT - 1 - c) * C, C),
                              **{"in": (qb_ref, kb_ref, vb_ref, gbk_ref), "out": ob_ref}))
        for stage in stages:
            for u in units:
                stage(u)
        return carry

    lax.fori_loop(0, DN_CT, body, 0)


def _dn(q, k, v, gb, masks, tri, eg, eb):
    nb, seq, _ = q.shape
    nbat = math.gcd(nb, DN_NB)
    rows = DN_CT * DN_CHUNK
    nt = seq // rows

    def fwd(width):
        return pl.BlockSpec((nbat, rows, width), lambda b, j: (b, j, 0))

    def bwd(width):
        return pl.BlockSpec((nbat, rows, width), lambda b, j: (b, nt - 1 - j, 0))

    out = jax.ShapeDtypeStruct((nb, seq, DN_WIDTH), BF16)
    return pl.pallas_call(
        functools.partial(_dn_kernel, nbat=nbat),
        grid=(nb // nbat, nt),
        in_specs=[fwd(DN_WIDTH), fwd(DN_WIDTH), fwd(DN_WIDTH), fwd(4 * DN_HEADS),
                  bwd(DN_WIDTH), bwd(DN_WIDTH), bwd(DN_WIDTH), bwd(4 * DN_HEADS),
                  _resident((2, 3, DN_CHUNK, DN_WIDTH)), _resident((2, DN_CHUNK, DN_CHUNK)),
                  _resident((2, 4 * DN_HEADS, DN_WIDTH)), _resident((2, 4 * DN_HEADS, DN_WIDTH))],
        out_specs=(fwd(DN_WIDTH), bwd(DN_WIDTH)),
        out_shape=(out, out),
        scratch_shapes=[pltpu.VMEM((2 * nbat, DN_CHUNK, DN_WIDTH), F32),
                        pltpu.VMEM((2 * nbat, DN_BD_BUFS, DN_WIDTH, DN_WIDTH), BF16)],
        compiler_params=_cparams("parallel", "arbitrary"),
        name="deltanet",
    )(q, k, v, gb, q, k, v, gb, masks, tri, eg, eb)


def _dn_masks():
    r = jnp.arange(DN_CHUNK)[:, None]
    c = (jnp.arange(DN_WIDTH) % DN_CHUNK)[None, :]
    fwd = jnp.stack([r >= c, r > c, r == c])
    bwd = jnp.stack([r <= c, r < c, r == c])
    masks = jnp.stack([fwd, bwd]).astype(F32)
    i = jnp.arange(DN_CHUNK)
    tri = jnp.stack([i[:, None] >= i[None, :], i[:, None] <= i[None, :]]).astype(BF16)
    return masks, tri


def _attn_kernel(qt_ref, k_ref, vt_ref, o_ref, qz_scr, m_scr, acc_scr):
    tq = qt_ref.shape[1]
    tk = k_ref.shape[0]
    kj = pl.program_id(2)
    gq = GROUP * tq

    @pl.when(kj == 0)
    def _():
        for h in range(N_HEADS):
            qh = qt_ref[h * HEAD_DIM:(h + 1) * HEAD_DIM, :]
            z = jnp.zeros_like(qh)
            qz_scr[:, h * tq:(h + 1) * tq] = jnp.concatenate([qh, z] if h < GROUP else [z, qh], axis=0)
        m_scr[...] = jnp.full_like(m_scr, -jnp.inf)
        acc_scr[...] = jnp.zeros_like(acc_scr)

    kblk = k_ref[...]
    vt = vt_ref[...]
    row_v = lax.broadcasted_iota(jnp.int32, (KV_WIDTH, tk), 0)
    for kv in range(N_KV_HEADS):
        cols = slice(kv * gq, (kv + 1) * gq)
        st = _mm(kblk, qz_scr[:, cols])
        m_old = m_scr[:, cols]
        m_new = jnp.maximum(m_old, jnp.max(st, axis=0, keepdims=True))
        alpha = jnp.exp2(m_old - m_new)[0:1, :]
        pt = jnp.exp2(st - m_new[0:1, :]).astype(BF16)
        own = (row_v < HEAD_DIM) if kv == 0 else (row_v >= HEAD_DIM)
        vx = jnp.where(own, vt, jnp.ones_like(vt))
        acc_scr[:, cols] = alpha * acc_scr[:, cols] + _mm(vx, pt)
        m_scr[:, cols] = m_new

    @pl.when(kj == pl.num_programs(2) - 1)
    def _():
        for kv in range(N_KV_HEADS):
            a = acc_scr[:, kv * gq:(kv + 1) * gq]
            if kv == 0:
                o = a[0:HEAD_DIM] / a[HEAD_DIM:2 * HEAD_DIM]
            else:
                o = a[HEAD_DIM:2 * HEAD_DIM] / a[0:HEAD_DIM]
            for j in range(GROUP // 2):
                pair = jnp.concatenate([o[:, (2 * j) * tq:(2 * j + 1) * tq], o[:, (2 * j + 1) * tq:(2 * j + 2) * tq]],
                                       axis=0)
                c0 = kv * GROUP * HEAD_DIM + j * LANES
                o_ref[:, c0:c0 + LANES] = pair.T.astype(BF16)


def _attention(qt, k, vt):
    nb, seq, _ = k.shape
    tq, tk = ATT_TQ, ATT_TK
    return pl.pallas_call(
        _attn_kernel,
        grid=(nb, seq // tq, seq // tk),
        in_specs=[pl.BlockSpec((None, ATTN_WIDTH, tq), lambda b, i, j: (b, 0, i)),
                  pl.BlockSpec((None, tk, KV_WIDTH), lambda b, i, j: (b, j, 0)),
                  pl.BlockSpec((None, KV_WIDTH, tk), lambda b, i, j: (b, 0, j))],
        out_specs=pl.BlockSpec((None, tq, ATTN_WIDTH), lambda b, i, j: (b, i, 0)),
        out_shape=jax.ShapeDtypeStruct((nb, seq, ATTN_WIDTH), BF16),
        scratch_shapes=[pltpu.VMEM((2 * HEAD_DIM, N_HEADS * tq), BF16), pltpu.VMEM((8, N_HEADS * tq), F32),
                        pltpu.VMEM((2 * HEAD_DIM, N_HEADS * tq), F32)],
        compiler_params=_cparams("parallel", "parallel", "arbitrary"),
        name="gqa_attention",
    )(qt, k, vt)


def _mem_kv_kernel(m_ref, nw_ref, wkv_ref, kt_ref, v_ref):
    mn = _rms(m_ref[...], nw_ref[...]).astype(BF16)
    k = _mm(mn, wkv_ref[:, 0:D_MODEL])
    kt_ref[...] = k.T.astype(BF16)
    v_ref[...] = _mm(mn, wkv_ref[:, D_MODEL:2 * D_MODEL]).astype(BF16)


def _mem_kv(mem, nw, wkv):
    nb = mem.shape[0]
    return pl.pallas_call(
        _mem_kv_kernel,
        grid=(nb,),
        in_specs=[pl.BlockSpec((None, MEM_TOKENS, D_MODEL), lambda b: (b, 0, 0)), _resident((1, D_MODEL)),
                  _resident((D_MODEL, 2 * D_MODEL))],
        out_specs=(pl.BlockSpec((None, D_MODEL, MEM_TOKENS), lambda b: (b, 0, 0)),
                   pl.BlockSpec((None, MEM_TOKENS, D_MODEL), lambda b: (b, 0, 0))),
        out_shape=(jax.ShapeDtypeStruct((nb, D_MODEL, MEM_TOKENS), BF16),
                   jax.ShapeDtypeStruct((nb, MEM_TOKENS, D_MODEL), BF16)),
        compiler_params=_cparams("parallel"),
        name="mem_kv",
    )(mem, nw, wkv)


def _out_mem_kernel(x_ref, yf_ref, of_ref, ob_ref, gate_ref, onw_ref, ones_ref, ya_ref, wout_ref, nx_ref, wq_ref,
                    kt_ref, v_ref, wo_ref, o_ref, o_scr):
    blk = yf_ref[...]
    yf = jnp.concatenate([blk[:, j * FOUR_WIDTH:(j + 1) * FOUR_WIDTH] for j in range(blk.shape[1] // FOUR_WIDTH)],
                         axis=0)
    od = of_ref[...].astype(F32) + ob_ref[...].astype(F32)
    g = gate_ref[...].astype(F32)
    yd = (od * lax.rsqrt(_group_mean_sq(od, ones_ref[...]) + EPS) * onw_ref[...]) * (g * jax.nn.sigmoid(g))
    x1 = (x_ref[...] + _mm(yf, wout_ref[0:FOUR_WIDTH, :])
          + _mm(yd.astype(BF16), wout_ref[FOUR_WIDTH:FOUR_WIDTH + DN_WIDTH, :])
          + _mm(ya_ref[...], wout_ref[FOUR_WIDTH + DN_WIDTH:D_MODEL, :]))
    h = _rms(x1, nx_ref[...]).astype(BF16)
    q = (_mm(h, wq_ref[...]) * (MEM_HEAD_DIM ** -0.5)).astype(BF16)
    for hh in range(MEM_HEADS):
        sl = slice(hh * MEM_HEAD_DIM, (hh + 1) * MEM_HEAD_DIM)
        s = _mm(q[:, sl], kt_ref[sl, :])
        p = jnp.exp(s - jnp.max(s, axis=1, keepdims=True))
        o = _mm(p.astype(BF16), v_ref[:, sl]) / jnp.sum(p, axis=1, keepdims=True)
        o_scr[:, sl] = o.astype(BF16)
    o_ref[...] = x1 + _mm(o_scr[...], wo_ref[...])


def _out_mem(x, seq, yf, of, ob, gate, onw, ones256, ya, wout, nx, wq, kt, v, wo):
    t = x.shape[0]
    tm = TOKEN_TILE
    per_seq = seq // tm
    rows = seq // GRID_W
    jw = (tm // rows) * FOUR_WIDTH

    def row(width):
        return pl.BlockSpec((tm, width), lambda i: (i, 0))

    return pl.pallas_call(
        _out_mem_kernel,
        grid=(t // tm,),
        in_specs=[row(D_MODEL),
                  pl.BlockSpec((None, rows, jw), lambda i: (i // per_seq, 0, i % per_seq)),
                  row(DN_WIDTH), row(DN_WIDTH), row(DN_WIDTH), _resident((1, DN_WIDTH)),
                  _resident((DN_WIDTH, DN_WIDTH)),
                  row(ATTN_WIDTH), _resident((D_MODEL, D_MODEL)), _resident((1, D_MODEL)),
                  _resident((D_MODEL, D_MODEL)),
                  pl.BlockSpec((None, D_MODEL, MEM_TOKENS), lambda i: (i // per_seq, 0, 0)),
                  pl.BlockSpec((None, MEM_TOKENS, D_MODEL), lambda i: (i // per_seq, 0, 0)),
                  _resident((D_MODEL, D_MODEL))],
        out_specs=row(D_MODEL),
        out_shape=jax.ShapeDtypeStruct((t, D_MODEL), F32),
        scratch_shapes=[pltpu.VMEM((tm, D_MODEL), BF16)],
        compiler_params=_cparams("parallel"),
        name="out_mem",
    )(x, yf, of, ob, gate, onw, ones256, ya, wout, nx, wq, kt, v, wo)


def _rope_tables(seq):
    rows = seq // GRID_W
    r = jnp.repeat(jnp.arange(rows), GRID_W).astype(F32)
    c = jnp.tile(jnp.arange(GRID_W), rows).astype(F32)
    inv = 1.0 / (ROPE_THETA ** (jnp.arange(ROPE_FREQS, dtype=F32) * (2.0 / AXIS_DIM)))
    ang = jnp.stack([r[:, None] * inv, c[:, None] * inv], axis=1)
    cos = jnp.cos(ang)[:, :, None, :]
    sin = jnp.sin(ang)[:, :, None, :]
    cos = jnp.broadcast_to(cos, (seq, 2, 2, ROPE_FREQS)).reshape(seq, HEAD_DIM)
    sin = jnp.concatenate([-sin, sin], axis=2).reshape(seq, HEAD_DIM)
    return jnp.tile(cos, (1, LANES // HEAD_DIM)), jnp.tile(sin, (1, LANES // HEAD_DIM))


def _ones_blockdiag(n, g):
    i = jnp.arange(n) // g
    return (i[:, None] == i[None, :]).astype(BF16)


def _head_expanders():
    src = jnp.arange(4 * DN_HEADS)[:, None]
    head = (jnp.arange(DN_WIDTH) // DN_HEAD_DIM)[None, :]
    eg = jnp.stack([(src == d * DN_HEADS + head) for d in range(2)]).astype(BF16)
    eb = jnp.stack([(src == 2 * DN_HEADS + d * DN_HEADS + head) for d in range(2)]).astype(BF16)
    return eg, eb


def _encoder(x, mem, lw, final_norm):
    nb, seq, _ = x.shape
    t = nb * seq
    rows = seq // GRID_W
    cos, sin = _rope_tables(seq)
    cs, tw, ch = _fourier_tables(seq)
    ones128 = _ones_blockdiag(LANES, HEAD_DIM)
    ones256 = _ones_blockdiag(DN_WIDTH, DN_HEAD_DIM)
    eg, eb = _head_expanders()
    dn_masks, dn_tri = _dn_masks()
    four_scale = 1.0 / math.sqrt(seq * FOUR_DIM)
    x2 = x.reshape(t, D_MODEL)
    for l in range(DEPTH):
        w = lw[l]
        x2, uf, qkv, gate, aqt, ak, avt, ab = _ffn_mix(x2, seq, w["n1"], w["wgu1"], w["wd1"], w["nm"], w["win"],
                                                      w["qw"], w["kw"], cos, sin, ones128)
        yre, yim = _four_rows(uf.reshape(nb, rows, GRID_W * FOUR_WIDTH), cs)
        yf = _four_cols(yre.reshape(nb, rows, GRID_W, FOUR_WIDTH), yim.reshape(nb, rows, GRID_W, FOUR_WIDTH),
                        tw, ch, w["fw"], four_scale)
        dq, dk, dv, gb = _dn_prep(qkv.reshape(nb, seq, 3 * DN_WIDTH), ab.reshape(nb, seq, 4 * DN_HEADS),
                                  w["cw"], w["na"], w["dtb"], ones256)
        of, ob = _dn(dq, dk, dv, gb, dn_masks, dn_tri, eg, eb)
        ya = _attention(aqt, ak.reshape(nb, seq, KV_WIDTH), avt)
        mkt, mv = _mem_kv(mem, w["nmm"], w["wkv"])
        x2 = _out_mem(x2, seq, yf.reshape(nb, rows, GRID_W * FOUR_WIDTH), of.reshape(t, DN_WIDTH),
                      ob.reshape(t, DN_WIDTH), gate, w["onw"], ones256, ya.reshape(t, ATTN_WIDTH),
                      w["wout"], w["nmx"], w["wq"], mkt, mv, w["wo"])
        x2 = _ffn(x2, w["n2"], w["wgu2"], w["wd2"], final_norm, final_norm=(l == DEPTH - 1))
    return x2.reshape(nb, seq, D_MODEL)


def _prep_weights(ffn1_norm, ffn1_w_gu, ffn1_w_down, mix_norm, w_in, four_w, dn_conv, dn_A_log, dn_dt_bias,
                  dn_out_norm, attn_q_norm, attn_k_norm, w_out, mem_norm_x, mem_norm_m, mem_wq, mem_wkv, mem_wo,
                  ffn2_norm, ffn2_w_gu, ffn2_w_down):
    lw = []
    o_ab = FOUR_WIDTH + 3 * DN_WIDTH
    o_gate = o_ab + 4 * DN_HEADS
    for l in range(DEPTH):
        wi = w_in[l]
        win = jnp.concatenate([wi[:, :o_ab], wi[:, o_gate:], wi[:, o_ab:o_gate],
                               jnp.zeros((D_MODEL, LANES - 4 * DN_HEADS), F32)], axis=1).astype(BF16)
        fw = jnp.zeros((FOUR_WIDTH, FOUR_WIDTH), F32)
        for g in range(FOUR_GROUPS):
            fw = lax.dynamic_update_slice(fw, four_w[l, g], (g * FOUR_DIM, g * FOUR_DIM))
        zeros8 = jnp.zeros((2 * DN_HEADS,), F32)
        lw.append(dict(
            n1=ffn1_norm[l].reshape(1, D_MODEL), wgu1=ffn1_w_gu[l].astype(BF16), wd1=ffn1_w_down[l].astype(BF16),
            nm=mix_norm[l].reshape(1, D_MODEL), win=win,
            qw=jnp.tile(attn_q_norm[l], LANES // HEAD_DIM).reshape(1, LANES),
            kw=jnp.tile(attn_k_norm[l], LANES // HEAD_DIM).reshape(1, LANES),
            fw=fw.astype(BF16), cw=dn_conv[l],
            na=jnp.concatenate([dn_A_log[l].reshape(-1), zeros8]).reshape(1, 4 * DN_HEADS),
            dtb=jnp.concatenate([dn_dt_bias[l].reshape(-1), zeros8]).reshape(1, 4 * DN_HEADS),
            onw=jnp.tile(dn_out_norm[l], DN_HEADS).reshape(1, DN_WIDTH),
            wout=w_out[l].astype(BF16), nmx=mem_norm_x[l].reshape(1, D_MODEL), nmm=mem_norm_m[l].reshape(1, D_MODEL),
            wq=mem_wq[l].astype(BF16), wkv=mem_wkv[l].astype(BF16), wo=mem_wo[l].astype(BF16),
            n2=ffn2_norm[l].reshape(1, D_MODEL), wgu2=ffn2_w_gu[l].astype(BF16), wd2=ffn2_w_down[l].astype(BF16),
        ))
    return lw


def kernel(x_prompt, x_sample, mem_prompt, mem_sample, ffn1_norm, ffn1_w_gu, ffn1_w_down, mix_norm, w_in, four_w, dn_conv, dn_A_log, dn_dt_bias, dn_out_norm, attn_q_norm, attn_k_norm, w_out, mem_norm_x, mem_norm_m, mem_wq, mem_wkv, mem_wo, ffn2_norm, ffn2_w_gu, ffn2_w_down, final_norm):
    lw = _prep_weights(ffn1_norm, ffn1_w_gu, ffn1_w_down, mix_norm, w_in, four_w, dn_conv, dn_A_log, dn_dt_bias,
                       dn_out_norm, attn_q_norm, attn_k_norm, w_out, mem_norm_x, mem_norm_m, mem_wq, mem_wkv, mem_wo,
                       ffn2_norm, ffn2_w_gu, ffn2_w_down)
    fn = final_norm.reshape(1, D_MODEL)
    return (_encoder(x_prompt, mem_prompt, lw, fn), _encoder(x_sample, mem_sample, lw, fn))
```

```python
import functools
import math

import jax
import jax.numpy as jnp
from jax import lax
from jax.experimental import pallas as pl
from jax.experimental.pallas import tpu as pltpu

F32 = jnp.float32
BF16 = jnp.bfloat16

D_MODEL = 1024
DEPTH = 2
GRID_W = 64
EPS = 1e-6
FOUR_GROUPS = 4
FOUR_DIM = 64
FOUR_WIDTH = FOUR_GROUPS * FOUR_DIM
DN_HEADS = 4
DN_HEAD_DIM = 64
DN_WIDTH = DN_HEADS * DN_HEAD_DIM
DN_CHUNK = 64
CONV_W = 3
N_HEADS = 8
N_KV_HEADS = 2
HEAD_DIM = 64
GROUP = N_HEADS // N_KV_HEADS
ATTN_WIDTH = N_HEADS * HEAD_DIM
KV_WIDTH = N_KV_HEADS * HEAD_DIM
ROPE_THETA = 10000.0
AXIS_DIM = HEAD_DIM // 2
ROPE_FREQS = AXIS_DIM // 2
MEM_TOKENS = 256
MEM_HEADS = 4
MEM_HEAD_DIM = D_MODEL // MEM_HEADS
D_FF = 2816

V7X_VMEM_LIMIT_BYTES = 56 * 1024 * 1024
LANES = 128

TOKEN_TILE = 512
FF_CHUNK = 256
ATT_TQ = 1024
ATT_TK = 512
FOUR_LANE_TILE = 2048
FOUR_T = 8
IN_MAIN = FOUR_WIDTH + 3 * DN_WIDTH + DN_WIDTH
IN_ATT = ATTN_WIDTH + 2 * KV_WIDTH
IN_PERM = IN_MAIN + IN_ATT + LANES
ATT_Q_SCALE = HEAD_DIM ** -0.5 * math.log2(math.e)


def _cparams(*sem):
    return pltpu.CompilerParams(dimension_semantics=sem, vmem_limit_bytes=V7X_VMEM_LIMIT_BYTES)


def _resident(shape):
    nd = len(shape)
    return pl.BlockSpec(shape, lambda *_: (0,) * nd, pipeline_mode=pl.Buffered(1))


def _dot(a, b, dims):
    return lax.dot_general(a, b, (dims, ((), ())), preferred_element_type=F32)


def _mm(a, b):
    return _dot(a, b, ((1,), (0,)))


def _split3(x):
    hi = x.astype(BF16)
    r = x - hi.astype(F32)
    mid = r.astype(BF16)
    lo = (r - mid.astype(F32)).astype(BF16)
    return hi, mid, lo


def _mm_exact_rhs(x, sel):
    hi, mid, lo = _split3(x)
    return _mm(hi, sel) + _mm(mid, sel) + _mm(lo, sel)


def _mm_exact_lhs(sel, x):
    hi, mid, lo = _split3(x)
    return _mm(sel, hi) + _mm(sel, mid) + _mm(sel, lo)


def _rms(x, w):
    ms = jnp.mean(x * x, axis=-1, keepdims=True)
    return x * lax.rsqrt(ms + EPS) * w


def _group_mean_sq(x, ones_bd):
    return _mm_exact_rhs(x * x, ones_bd) * (1.0 / 64.0)


def _ffn_core(x, nw_ref, wgu_ref, wd_ref, h_scr):
    n = _rms(x, nw_ref[...]).astype(BF16)
    for c in range(D_FF // FF_CHUNK):
        lo = c * FF_CHUNK
        g = _mm(n, wgu_ref[:, lo:lo + FF_CHUNK])
        u = _mm(n, wgu_ref[:, D_FF + lo:D_FF + lo + FF_CHUNK])
        h_scr[:, lo:lo + FF_CHUNK] = (g * jax.nn.sigmoid(g) * u).astype(BF16)
    return x + 0.5 * _mm(h_scr[...], wd_ref[...])


def _ffn_kernel(x_ref, nw_ref, wgu_ref, wd_ref, fw_ref, o_ref, h_scr, *, final_norm):
    y = _ffn_core(x_ref[...], nw_ref, wgu_ref, wd_ref, h_scr)
    if final_norm:
        y = _rms(y, fw_ref[...])
    o_ref[...] = y


def _ffn(x, nw, wgu, wd, fw, final_norm):
    t = x.shape[0]
    tm = TOKEN_TILE
    row = pl.BlockSpec((tm, D_MODEL), lambda i: (i, 0))
    return pl.pallas_call(
        functools.partial(_ffn_kernel, final_norm=final_norm),
        grid=(t // tm,),
        in_specs=[row, _resident((1, D_MODEL)), _resident((D_MODEL, 2 * D_FF)), _resident((D_FF, D_MODEL)),
                  _resident((1, D_MODEL))],
        out_specs=row,
        out_shape=jax.ShapeDtypeStruct((t, D_MODEL), F32),
        scratch_shapes=[pltpu.VMEM((tm, D_FF), BF16)],
        compiler_params=_cparams("parallel"),
        name="ffn",
    )(x, nw, wgu, wd, fw)


def _ffn_mix_kernel(x_ref, n1_ref, wgu_ref, wd_ref, nm_ref, win_ref, qw_ref, kw_ref, cos_ref, sin_ref, ones_ref,
                    xo_ref, uf_ref, qkv_ref, gate_ref, aqt_ref, ak_ref, avt_ref, ab_ref, h_scr):
    x1 = _ffn_core(x_ref[...], n1_ref, wgu_ref, wd_ref, h_scr)
    xo_ref[...] = x1
    n = _rms(x1, nm_ref[...]).astype(BF16)
    z1 = _mm(n, win_ref[:, 0:IN_MAIN])
    uf_ref[...] = z1[:, 0:FOUR_WIDTH].astype(BF16)
    qkv_ref[...] = z1[:, FOUR_WIDTH:FOUR_WIDTH + 3 * DN_WIDTH].astype(BF16)
    gate_ref[...] = z1[:, FOUR_WIDTH + 3 * DN_WIDTH:IN_MAIN].astype(BF16)

    z2 = _mm(n, win_ref[:, IN_MAIN:IN_MAIN + IN_ATT])
    avt_ref[...] = z2[:, ATTN_WIDTH + KV_WIDTH:IN_ATT].T.astype(BF16)
    cos = cos_ref[...]
    sin = sin_ref[...]
    ones_bd = ones_ref[...]
    lane = lax.broadcasted_iota(jnp.int32, cos.shape, 1)
    first_half = (lane & (AXIS_DIM - 1)) < ROPE_FREQS

    def norm_rope(xb, w):
        y = xb * lax.rsqrt(_group_mean_sq(xb, ones_bd) + EPS) * w
        partner = jnp.where(first_half, pltpu.roll(y, LANES - ROPE_FREQS, 1), pltpu.roll(y, ROPE_FREQS, 1))
        return y * cos + partner * sin

    qw = qw_ref[...]
    for j in range(ATTN_WIDTH // LANES):
        qj = norm_rope(z2[:, j * LANES:(j + 1) * LANES], qw) * ATT_Q_SCALE
        aqt_ref[j * LANES:(j + 1) * LANES, :] = qj.T.astype(BF16)
    ak_ref[...] = norm_rope(z2[:, ATTN_WIDTH:ATTN_WIDTH + KV_WIDTH], kw_ref[...]).astype(BF16)

    z3 = _mm(n, win_ref[:, IN_MAIN + IN_ATT:IN_PERM])
    ab_ref[...] = z3[:, 0:4 * DN_HEADS]


def _ffn_mix(x, seq, n1, wgu, wd, nm, win, qw, kw, cos, sin, ones128):
    t = x.shape[0]
    tm = TOKEN_TILE
    nb = t // seq
    per_seq = seq // tm

    def row(width):
        return pl.BlockSpec((tm, width), lambda i: (i, 0))

    tab = pl.BlockSpec((tm, LANES), lambda i: (i % per_seq, 0))
    outs = (
        jax.ShapeDtypeStruct((t, D_MODEL), F32),
        jax.ShapeDtypeStruct((t, FOUR_WIDTH), BF16),
        jax.ShapeDtypeStruct((t, 3 * DN_WIDTH), BF16),
        jax.ShapeDtypeStruct((t, DN_WIDTH), BF16),
        jax.ShapeDtypeStruct((nb, ATTN_WIDTH, seq), BF16),
        jax.ShapeDtypeStruct((t, KV_WIDTH), BF16),
        jax.ShapeDtypeStruct((nb, KV_WIDTH, seq), BF16),
        jax.ShapeDtypeStruct((t, 4 * DN_HEADS), F32),
    )

    def col(height):
        return pl.BlockSpec((None, height, tm), lambda i: (i // per_seq, 0, i % per_seq))

    out_specs = (
        row(D_MODEL), row(FOUR_WIDTH), row(3 * DN_WIDTH), row(DN_WIDTH), col(ATTN_WIDTH),
        row(KV_WIDTH), col(KV_WIDTH), row(4 * DN_HEADS),
    )
    return pl.pallas_call(
        _ffn_mix_kernel,
        grid=(t // tm,),
        in_specs=[row(D_MODEL), _resident((1, D_MODEL)), _resident((D_MODEL, 2 * D_FF)), _resident((D_FF, D_MODEL)),
                  _resident((1, D_MODEL)), _resident((D_MODEL, IN_PERM)), _resident((1, LANES)), _resident((1, LANES)),
                  tab, tab, _resident((LANES, LANES))],
        out_specs=out_specs,
        out_shape=outs,
        scratch_shapes=[pltpu.VMEM((tm, D_FF), BF16)],
        compiler_params=_cparams("parallel"),
        name="ffn_mix",
    )(x, n1, wgu, wd, nm, win, qw, kw, cos, sin, ones128)


def _four_rows_kernel(x_ref, cs_ref, re_ref, im_ref, *, rows):
    y = _mm(cs_ref[...], x_ref[...])
    re_ref[...] = y[0:rows].astype(BF16)
    im_ref[...] = y[rows:2 * rows].astype(BF16)


def _four_rows(u, cs):
    nb, rows, width = u.shape
    tn = FOUR_LANE_TILE
    blk = pl.BlockSpec((None, rows, tn), lambda b, j: (b, 0, j))
    out = jax.ShapeDtypeStruct((nb, rows, width), BF16)
    return pl.pallas_call(
        functools.partial(_four_rows_kernel, rows=rows),
        grid=(nb, width // tn),
        in_specs=[blk, _resident((2 * rows, rows))],
        out_specs=(blk, blk),
        out_shape=(out, out),
        compiler_params=_cparams("parallel", "parallel"),
        name="four_rows",
    )(u, cs)


def _four_cols_kernel(re_ref, im_ref, tw_ref, ch_ref, w_ref, o_ref, ore_scr, oim_scr, *, scale):
    for t in range(FOUR_T):
        rhs = jnp.concatenate([re_ref[t], im_ref[t]], axis=0)
        o = _mm(tw_ref[t], rhs)
        ore_scr[t * GRID_W:(t + 1) * GRID_W, :] = o[0:GRID_W].astype(BF16)
        oim_scr[t * GRID_W:(t + 1) * GRID_W, :] = o[GRID_W:2 * GRID_W].astype(BF16)
    f = _mm(ore_scr[...], ch_ref[0:FOUR_WIDTH, :]) + _mm(oim_scr[...], ch_ref[FOUR_WIDTH:2 * FOUR_WIDTH, :])
    y = _mm((f * scale).astype(BF16), w_ref[...])
    o_ref[...] = y.reshape(FOUR_T, GRID_W, FOUR_WIDTH).astype(BF16)


def _four_cols(yre, yim, tw, ch, wbd, scale):
    nb, rows = yre.shape[0], yre.shape[1]
    blk = pl.BlockSpec((None, FOUR_T, GRID_W, FOUR_WIDTH), lambda b, i: (b, i, 0, 0))
    return pl.pallas_call(
        functools.partial(_four_cols_kernel, scale=scale),
        grid=(nb, rows // FOUR_T),
        in_specs=[blk, blk, pl.BlockSpec((FOUR_T, 2 * GRID_W, 2 * GRID_W), lambda b, i: (i, 0, 0)),
                  _resident((2 * FOUR_WIDTH, FOUR_WIDTH)), _resident((FOUR_WIDTH, FOUR_WIDTH))],
        out_specs=blk,
        out_shape=jax.ShapeDtypeStruct((nb, rows, GRID_W, FOUR_WIDTH), BF16),
        scratch_shapes=[pltpu.VMEM((FOUR_T * GRID_W, FOUR_WIDTH), BF16)] * 2,
        compiler_params=_cparams("parallel", "parallel"),
        name="four_cols",
    )(yre, yim, tw, ch, wbd)


def _fourier_tables(seq):
    rows = seq // GRID_W
    two_pi = 2.0 * math.pi
    kr = jnp.arange(rows, dtype=jnp.int32)
    ang_r = ((kr[:, None] * kr[None, :]) % rows).astype(F32) * (two_pi / rows)
    cs = jnp.concatenate([jnp.cos(ang_r), -jnp.sin(ang_r)], axis=0).astype(BF16)
    c = jnp.arange(GRID_W, dtype=jnp.int32)
    k = rows * c[None, :, None] + kr[:, None, None]
    ang = ((k * c[None, None, :]) % seq).astype(F32) * (two_pi / seq)
    cg, sg = jnp.cos(ang), jnp.sin(ang)
    tw = jnp.concatenate([jnp.concatenate([cg, sg], axis=2), jnp.concatenate([-sg, cg], axis=2)], axis=1).astype(BF16)
    ang_c = ((c[:, None] * c[None, :]) % FOUR_DIM).astype(F32) * (two_pi / FOUR_DIM)
    eye = jnp.eye(FOUR_GROUPS, dtype=F32)
    ch = jnp.concatenate([jnp.kron(eye, jnp.cos(ang_c)), jnp.kron(eye, jnp.sin(ang_c))], axis=0).astype(BF16)
    return cs, tw, ch


DN_HALO = 16


def _dn_prep_kernel(x_ref, prev_ref, next_ref, ab_ref, cw_ref, na_ref, dtb_ref, ones_ref,
                    q_ref, k_ref, v_ref, gb_ref, *, tiles_per_seq):
    i = pl.program_id(1)
    x = x_ref[...].astype(F32)
    ts = x.shape[0]
    has_prev = jnp.where(i > 0, 1.0, 0.0)
    has_next = jnp.where(i < tiles_per_seq - 1, 1.0, 0.0)
    prev_row = prev_ref[...].astype(F32)[DN_HALO - 1:DN_HALO, :] * has_prev
    next_row = next_ref[...].astype(F32)[0:1, :] * has_next
    r = lax.broadcasted_iota(jnp.int32, x.shape, 0)
    xm = jnp.where(r == 0, prev_row, pltpu.roll(x, 1, 0))
    xp = jnp.where(r == ts - 1, next_row, pltpu.roll(x, ts - 1, 0))
    cw = cw_ref[...]
    y = xm * cw[0:1, :] + x * cw[1:2, :] + xp * cw[2:3, :]
    y = y * jax.nn.sigmoid(y)
    ones_bd = ones_ref[...]

    def l2n(z):
        return z * lax.rsqrt(_group_mean_sq(z, ones_bd) * float(DN_HEAD_DIM) + EPS)

    q_ref[...] = (l2n(y[:, 0:DN_WIDTH]) * (DN_HEAD_DIM ** -0.5)).astype(BF16)
    k_ref[...] = l2n(y[:, DN_WIDTH:2 * DN_WIDTH]).astype(BF16)
    v_ref[...] = y[:, 2 * DN_WIDTH:3 * DN_WIDTH].astype(BF16)

    t = ab_ref[...] + dtb_ref[...]
    softplus = jnp.maximum(t, 0.0) + jnp.log1p(jnp.exp(-jnp.abs(t)))
    lane = lax.broadcasted_iota(jnp.int32, t.shape, 1)
    gb_ref[...] = jnp.where(lane < 2 * DN_HEADS, -jnp.exp(na_ref[...]) * softplus, jax.nn.sigmoid(t))


def _dn_prep(qkv, ab, cw, na, dtb, ones256):
    nb, seq, _ = qkv.shape
    ts = TOKEN_TILE
    tiles = seq // ts
    hpt = ts // DN_HALO
    n_halo = seq // DN_HALO

    def blk(width):
        return pl.BlockSpec((None, ts, width), lambda b, i: (b, i, 0))

    out = jax.ShapeDtypeStruct((nb, seq, DN_WIDTH), BF16)
    return pl.pallas_call(
        functools.partial(_dn_prep_kernel, tiles_per_seq=tiles),
        grid=(nb, tiles),
        in_specs=[blk(3 * DN_WIDTH),
                  pl.BlockSpec((None, DN_HALO, 3 * DN_WIDTH), lambda b, i: (b, jnp.maximum(i * hpt - 1, 0), 0)),
                  pl.BlockSpec((None, DN_HALO, 3 * DN_WIDTH), lambda b, i: (b, jnp.minimum((i + 1) * hpt, n_halo - 1), 0)),
                  blk(4 * DN_HEADS), _resident((CONV_W, 3 * DN_WIDTH)), _resident((1, 4 * DN_HEADS)),
                  _resident((1, 4 * DN_HEADS)), _resident((DN_WIDTH, DN_WIDTH))],
        out_specs=(blk(DN_WIDTH), blk(DN_WIDTH), blk(DN_WIDTH), blk(4 * DN_HEADS)),
        out_shape=(out, out, out, jax.ShapeDtypeStruct((nb, seq, 4 * DN_HEADS), F32)),
        compiler_params=_cparams("parallel", "parallel"),
        name="dn_prep",
    )(qkv, qkv, qkv, ab, cw, na, dtb, ones256)


DN_NB = 4
DN_CT = 8
DN_BD_BUFS = 7


def _dn_kernel(qf_ref, kf_ref, vf_ref, gf_ref, qb_ref, kb_ref, vb_ref, gbk_ref, mask_ref, tri_ref, eg_ref, eb_ref,
               of_ref, ob_ref, s_scr, bd_scr, *, nbat):
    C = DN_CHUNK
    j = pl.program_id(1)

    @pl.when(j == 0)
    def _():
        s_scr[...] = jnp.zeros_like(s_scr)
        bd_scr[...] = jnp.zeros_like(bd_scr)

    low_half = lax.broadcasted_iota(jnp.int32, (C, LANES), 1) < DN_HEAD_DIM
    nt = (((1,), (1,)))

    def set_bd(slot, buf, y):
        yb = y.astype(BF16)
        for h in range(DN_HEADS):
            t = h // 2
            tile = yb[:, t * LANES:(t + 1) * LANES]
            keep = low_half if h % 2 == 0 else jnp.logical_not(low_half)
            bd_scr[slot, buf, h * C:(h + 1) * C, t * LANES:(t + 1) * LANES] = jnp.where(keep, tile,
                                                                                        jnp.zeros_like(tile))
        return bd_scr[slot, buf]

    def prod(x, ybd):
        return _mm(x.astype(BF16), ybd)

    def st_load(u):
        d, bb, r0 = u["d"], u["bb"], u["r0"]
        q_ref, k_ref, v_ref, g_ref = u["in"]
        q = q_ref[bb, pl.ds(r0, C), :].astype(F32)
        k = k_ref[bb, pl.ds(r0, C), :].astype(F32)
        v = v_ref[bb, pl.ds(r0, C), :].astype(F32)
        gb = g_ref[bb, pl.ds(r0, C), :]
        u.update(q=q, k=k, v=v, beta=_mm_exact_rhs(gb, eb_ref[d]),
                 gcum=_mm_exact_lhs(tri_ref[d], gb))

    def st_decay(u):
        u["gc"] = _mm_exact_rhs(u["gcum"], eg_ref[u["d"]])

    def st_scale(u):
        d, gc, q, k = u["d"], u["gc"], u["q"], u["k"]
        u["gct"] = jnp.sum(gc * mask_ref[d, 2], axis=0, keepdims=True)
        u["g_last"] = gc[C - 1:C, :] if d == 0 else gc[0:1, :]
        kb = k * u["beta"]
        egc = jnp.exp(gc)
        u.update(kb=kb, vbeta=u["v"] * u["beta"], kbe=kb * egc, qd=q * egc)

    def st_gram(u):
        d = u["d"]
        incl, strict, eye = mask_ref[d, 0], mask_ref[d, 1], mask_ref[d, 2]
        r1 = _dot(jnp.concatenate([u["kb"], u["q"], eye], axis=0).astype(BF16), set_bd(u["slot"], 0, u["k"]), nt)
        decay = jnp.exp(jnp.minimum(u["gc"] - u["gct"], 0.0))
        low = r1[0:C] * (decay * strict)
        u["qkm"] = r1[C:2 * C] * (decay * incl)
        u["kt_tail"] = r1[2 * C:3 * C] * jnp.exp(u["g_last"] - u["gct"])
        u["inv"] = eye - low
        u["power"] = low
        u["lbd"] = set_bd(u["slot"], 1, low)

    def st_square(i):
        def f(u):
            u["power"] = prod(u["power"], u["lbd"])
            u["lbd"] = set_bd(u["slot"], 2 - i % 2, u["power"])
        return f

    def st_accum(u):
        u["inv"] = u["inv"] + prod(u["inv"], u["lbd"])

    def st_solve(u):
        u["u"] = prod(u["inv"], set_bd(u["slot"], 3, u["vbeta"]))
        u["w"] = prod(u["inv"], set_bd(u["slot"], 4, u["kbe"]))

    def st_state_in(u):
        state = s_scr[u["slot"]]
        ws = _mm(jnp.concatenate([u["w"], u["qd"]], axis=0).astype(BF16), set_bd(u["slot"], 5, state))
        u["state"] = state
        u["v_new"] = u["u"] - ws[0:C]
        u["o"] = ws[C:2 * C]

    def st_state_out(u):
        r2 = _mm(jnp.concatenate([u["qkm"], u["kt_tail"]], axis=0).astype(BF16), set_bd(u["slot"], 6, u["v_new"]))
        s_scr[u["slot"]] = u["state"] * jnp.exp(u["g_last"]) + r2[C:2 * C]
        u["out"][u["bb"], pl.ds(u["r0"], C), :] = (u["o"] + r2[0:C]).astype(BF16)

    stages = [st_load, st_decay, st_scale, st_gram]
    for i in range(5):
        stages += [st_square(i), st_accum]
    stages += [st_solve, st_state_in, st_state_out]

    def body(c, carry):
        units = []
        for bb in range(nbat):
            units.append(dict(d=0, bb=bb, slot=2 * bb, r0=pl.multiple_of(c * C, C),
                              **{"in": (qf_ref, kf_ref, vf_ref, gf_ref), "out": of_ref}))
            units.append(dict(d=1, bb=bb, slot=2 * bb + 1, r0=pl.multiple_of((DN_CT - 1 - c) * C, C),
                              **{"in": (qb_ref, kb_ref, vb_ref, gbk_ref), "out": ob_ref}))
        for stage in stages:
            for u in units:
                stage(u)
        return carry

    lax.fori_loop(0, DN_CT, body, 0)


def _dn(q, k, v, gb, masks, tri, eg, eb):
    nb, seq, _ = q.shape
    nbat = math.gcd(nb, DN_NB)
    rows = DN_CT * DN_CHUNK
    nt = seq // rows

    def fwd(width):
        return pl.BlockSpec((nbat, rows, width), lambda b, j: (b, j, 0))

    def bwd(width):
        return pl.BlockSpec((nbat, rows, width), lambda b, j: (b, nt - 1 - j, 0))

    out = jax.ShapeDtypeStruct((nb, seq, DN_WIDTH), BF16)
    return pl.pallas_call(
        functools.partial(_dn_kernel, nbat=nbat),
        grid=(nb // nbat, nt),
        in_specs=[fwd(DN_WIDTH), fwd(DN_WIDTH), fwd(DN_WIDTH), fwd(4 * DN_HEADS),
                  bwd(DN_WIDTH), bwd(DN_WIDTH), bwd(DN_WIDTH), bwd(4 * DN_HEADS),
                  _resident((2, 3, DN_CHUNK, DN_WIDTH)), _resident((2, DN_CHUNK, DN_CHUNK)),
                  _resident((2, 4 * DN_HEADS, DN_WIDTH)), _resident((2, 4 * DN_HEADS, DN_WIDTH))],
        out_specs=(fwd(DN_WIDTH), bwd(DN_WIDTH)),
        out_shape=(out, out),
        scratch_shapes=[pltpu.VMEM((2 * nbat, DN_CHUNK, DN_WIDTH), F32),
                        pltpu.VMEM((2 * nbat, DN_BD_BUFS, DN_WIDTH, DN_WIDTH), BF16)],
        compiler_params=_cparams("parallel", "arbitrary"),
        name="deltanet",
    )(q, k, v, gb, q, k, v, gb, masks, tri, eg, eb)


def _dn_masks():
    r = jnp.arange(DN_CHUNK)[:, None]
    c = (jnp.arange(DN_WIDTH) % DN_CHUNK)[None, :]
    fwd = jnp.stack([r >= c, r > c, r == c])
    bwd = jnp.stack([r <= c, r < c, r == c])
    masks = jnp.stack([fwd, bwd]).astype(F32)
    i = jnp.arange(DN_CHUNK)
    tri = jnp.stack([i[:, None] >= i[None, :], i[:, None] <= i[None, :]]).astype(BF16)
    return masks, tri


def _attn_kernel(qt_ref, k_ref, vt_ref, o_ref, qz_scr, m_scr, acc_scr):
    tq = qt_ref.shape[1]
    tk = k_ref.shape[0]
    kj = pl.program_id(2)
    gq = GROUP * tq

    @pl.when(kj == 0)
    def _():
        for h in range(N_HEADS):
            qh = qt_ref[h * HEAD_DIM:(h + 1) * HEAD_DIM, :]
            z = jnp.zeros_like(qh)
            qz_scr[:, h * tq:(h + 1) * tq] = jnp.concatenate([qh, z] if h < GROUP else [z, qh], axis=0)
        m_scr[...] = jnp.full_like(m_scr, -jnp.inf)
        acc_scr[...] = jnp.zeros_like(acc_scr)

    kblk = k_ref[...]
    vt = vt_ref[...]
    row_v = lax.broadcasted_iota(jnp.int32, (KV_WIDTH, tk), 0)
    for kv in range(N_KV_HEADS):
        cols = slice(kv * gq, (kv + 1) * gq)
        st = _mm(kblk, qz_scr[:, cols])
        m_old = m_scr[:, cols]
        m_new = jnp.maximum(m_old, jnp.max(st, axis=0, keepdims=True))
        alpha = jnp.exp2(m_old - m_new)[0:1, :]
        pt = jnp.exp2(st - m_new[0:1, :]).astype(BF16)
        own = (row_v < HEAD_DIM) if kv == 0 else (row_v >= HEAD_DIM)
        vx = jnp.where(own, vt, jnp.ones_like(vt))
        acc_scr[:, cols] = alpha * acc_scr[:, cols] + _mm(vx, pt)
        m_scr[:, cols] = m_new

    @pl.when(kj == pl.num_programs(2) - 1)
    def _():
        for kv in range(N_KV_HEADS):
            a = acc_scr[:, kv * gq:(kv + 1) * gq]
            if kv == 0:
                o = a[0:HEAD_DIM] / a[HEAD_DIM:2 * HEAD_DIM]
            else:
                o = a[HEAD_DIM:2 * HEAD_DIM] / a[0:HEAD_DIM]
            for j in range(GROUP // 2):
                pair = jnp.concatenate([o[:, (2 * j) * tq:(2 * j + 1) * tq], o[:, (2 * j + 1) * tq:(2 * j + 2) * tq]],
                                       axis=0)
                c0 = kv * GROUP * HEAD_DIM + j * LANES
                o_ref[:, c0:c0 + LANES] = pair.T.astype(BF16)


def _attention(qt, k, vt):
    nb, seq, _ = k.shape
    tq, tk = ATT_TQ, ATT_TK
    return pl.pallas_call(
        _attn_kernel,
        grid=(nb, seq // tq, seq // tk),
        in_specs=[pl.BlockSpec((None, ATTN_WIDTH, tq), lambda b, i, j: (b, 0, i)),
                  pl.BlockSpec((None, tk, KV_WIDTH), lambda b, i, j: (b, j, 0)),
                  pl.BlockSpec((None, KV_WIDTH, tk), lambda b, i, j: (b, 0, j))],
        out_specs=pl.BlockSpec((None, tq, ATTN_WIDTH), lambda b, i, j: (b, i, 0)),
        out_shape=jax.ShapeDtypeStruct((nb, seq, ATTN_WIDTH), BF16),
        scratch_shapes=[pltpu.VMEM((2 * HEAD_DIM, N_HEADS * tq), BF16), pltpu.VMEM((8, N_HEADS * tq), F32),
                        pltpu.VMEM((2 * HEAD_DIM, N_HEADS * tq), F32)],
        compiler_params=_cparams("parallel", "parallel", "arbitrary"),
        name="gqa_attention",
    )(qt, k, vt)


def _mem_kv_kernel(m_ref, nw_ref, wkv_ref, kt_ref, v_ref):
    mn = _rms(m_ref[...], nw_ref[...]).astype(BF16)
    k = _mm(mn, wkv_ref[:, 0:D_MODEL])
    kt_ref[...] = k.T.astype(BF16)
    v_ref[...] = _mm(mn, wkv_ref[:, D_MODEL:2 * D_MODEL]).astype(BF16)


def _mem_kv(mem, nw, wkv):
    nb = mem.shape[0]
    return pl.pallas_call(
        _mem_kv_kernel,
        grid=(nb,),
        in_specs=[pl.BlockSpec((None, MEM_TOKENS, D_MODEL), lambda b: (b, 0, 0)), _resident((1, D_MODEL)),
                  _resident((D_MODEL, 2 * D_MODEL))],
        out_specs=(pl.BlockSpec((None, D_MODEL, MEM_TOKENS), lambda b: (b, 0, 0)),
                   pl.BlockSpec((None, MEM_TOKENS, D_MODEL), lambda b: (b, 0, 0))),
        out_shape=(jax.ShapeDtypeStruct((nb, D_MODEL, MEM_TOKENS), BF16),
                   jax.ShapeDtypeStruct((nb, MEM_TOKENS, D_MODEL), BF16)),
        compiler_params=_cparams("parallel"),
        name="mem_kv",
    )(mem, nw, wkv)


def _out_mem_kernel(x_ref, yf_ref, of_ref, ob_ref, gate_ref, onw_ref, ones_ref, ya_ref, wout_ref, nx_ref, wq_ref,
                    kt_ref, v_ref, wo_ref, o_ref, o_scr):
    blk = yf_ref[...]
    yf = jnp.concatenate([blk[:, j * FOUR_WIDTH:(j + 1) * FOUR_WIDTH] for j in range(blk.shape[1] // FOUR_WIDTH)],
                         axis=0)
    od = of_ref[...].astype(F32) + ob_ref[...].astype(F32)
    g = gate_ref[...].astype(F32)
    yd = (od * lax.rsqrt(_group_mean_sq(od, ones_ref[...]) + EPS) * onw_ref[...]) * (g * jax.nn.sigmoid(g))
    x1 = (x_ref[...] + _mm(yf, wout_ref[0:FOUR_WIDTH, :])
          + _mm(yd.astype(BF16), wout_ref[FOUR_WIDTH:FOUR_WIDTH + DN_WIDTH, :])
          + _mm(ya_ref[...], wout_ref[FOUR_WIDTH + DN_WIDTH:D_MODEL, :]))
    h = _rms(x1, nx_ref[...]).astype(BF16)
    q = (_mm(h, wq_ref[...]) * (MEM_HEAD_DIM ** -0.5)).astype(BF16)
    for hh in range(MEM_HEADS):
        sl = slice(hh * MEM_HEAD_DIM, (hh + 1) * MEM_HEAD_DIM)
        s = _mm(q[:, sl], kt_ref[sl, :])
        p = jnp.exp(s - jnp.max(s, axis=1, keepdims=True))
        o = _mm(p.astype(BF16), v_ref[:, sl]) / jnp.sum(p, axis=1, keepdims=True)
        o_scr[:, sl] = o.astype(BF16)
    o_ref[...] = x1 + _mm(o_scr[...], wo_ref[...])


def _out_mem(x, seq, yf, of, ob, gate, onw, ones256, ya, wout, nx, wq, kt, v, wo):
    t = x.shape[0]
    tm = TOKEN_TILE
    per_seq = seq // tm
    rows = seq // GRID_W
    jw = (tm // rows) * FOUR_WIDTH

    def row(width):
        return pl.BlockSpec((tm, width), lambda i: (i, 0))

    return pl.pallas_call(
        _out_mem_kernel,
        grid=(t // tm,),
        in_specs=[row(D_MODEL),
                  pl.BlockSpec((None, rows, jw), lambda i: (i // per_seq, 0, i % per_seq)),
                  row(DN_WIDTH), row(DN_WIDTH), row(DN_WIDTH), _resident((1, DN_WIDTH)),
                  _resident((DN_WIDTH, DN_WIDTH)),
                  row(ATTN_WIDTH), _resident((D_MODEL, D_MODEL)), _resident((1, D_MODEL)),
                  _resident((D_MODEL, D_MODEL)),
                  pl.BlockSpec((None, D_MODEL, MEM_TOKENS), lambda i: (i // per_seq, 0, 0)),
                  pl.BlockSpec((None, MEM_TOKENS, D_MODEL), lambda i: (i // per_seq, 0, 0)),
                  _resident((D_MODEL, D_MODEL))],
        out_specs=row(D_MODEL),
        out_shape=jax.ShapeDtypeStruct((t, D_MODEL), F32),
        scratch_shapes=[pltpu.VMEM((tm, D_MODEL), BF16)],
        compiler_params=_cparams("parallel"),
        name="out_mem",
    )(x, yf, of, ob, gate, onw, ones256, ya, wout, nx, wq, kt, v, wo)


def _rope_tables(seq):
    rows = seq // GRID_W
    r = jnp.repeat(jnp.arange(rows), GRID_W).astype(F32)
    c = jnp.tile(jnp.arange(GRID_W), rows).astype(F32)
    inv = 1.0 / (ROPE_THETA ** (jnp.arange(ROPE_FREQS, dtype=F32) * (2.0 / AXIS_DIM)))
    ang = jnp.stack([r[:, None] * inv, c[:, None] * inv], axis=1)
    cos = jnp.cos(ang)[:, :, None, :]
    sin = jnp.sin(ang)[:, :, None, :]
    cos = jnp.broadcast_to(cos, (seq, 2, 2, ROPE_FREQS)).reshape(seq, HEAD_DIM)
    sin = jnp.concatenate([-sin, sin], axis=2).reshape(seq, HEAD_DIM)
    return jnp.tile(cos, (1, LANES // HEAD_DIM)), jnp.tile(sin, (1, LANES // HEAD_DIM))


def _ones_blockdiag(n, g):
    i = jnp.arange(n) // g
    return (i[:, None] == i[None, :]).astype(BF16)


def _head_expanders():
    src = jnp.arange(4 * DN_HEADS)[:, None]
    head = (jnp.arange(DN_WIDTH) // DN_HEAD_DIM)[None, :]
    eg = jnp.stack([(src == d * DN_HEADS + head) for d in range(2)]).astype(BF16)
    eb = jnp.stack([(src == 2 * DN_HEADS + d * DN_HEADS + head) for d in range(2)]).astype(BF16)
    return eg, eb


def _encoder(x, mem, lw, final_norm):
    nb, seq, _ = x.shape
    t = nb * seq
    rows = seq // GRID_W
    cos, sin = _rope_tables(seq)
    cs, tw, ch = _fourier_tables(seq)
    ones128 = _ones_blockdiag(LANES, HEAD_DIM)
    ones256 = _ones_blockdiag(DN_WIDTH, DN_HEAD_DIM)
    eg, eb = _head_expanders()
    dn_masks, dn_tri = _dn_masks()
    four_scale = 1.0 / math.sqrt(seq * FOUR_DIM)
    x2 = x.reshape(t, D_MODEL)
    for l in range(DEPTH):
        w = lw[l]
        x2, uf, qkv, gate, aqt, ak, avt, ab = _ffn_mix(x2, seq, w["n1"], w["wgu1"], w["wd1"], w["nm"], w["win"],
                                                      w["qw"], w["kw"], cos, sin, ones128)
        yre, yim = _four_rows(uf.reshape(nb, rows, GRID_W * FOUR_WIDTH), cs)
        yf = _four_cols(yre.reshape(nb, rows, GRID_W, FOUR_WIDTH), yim.reshape(nb, rows, GRID_W, FOUR_WIDTH),
                        tw, ch, w["fw"], four_scale)
        dq, dk, dv, gb = _dn_prep(qkv.reshape(nb, seq, 3 * DN_WIDTH), ab.reshape(nb, seq, 4 * DN_HEADS),
                                  w["cw"], w["na"], w["dtb"], ones256)
        of, ob = _dn(dq, dk, dv, gb, dn_masks, dn_tri, eg, eb)
        ya = _attention(aqt, ak.reshape(nb, seq, KV_WIDTH), avt)
        mkt, mv = _mem_kv(mem, w["nmm"], w["wkv"])
        x2 = _out_mem(x2, seq, yf.reshape(nb, rows, GRID_W * FOUR_WIDTH), of.reshape(t, DN_WIDTH),
                      ob.reshape(t, DN_WIDTH), gate, w["onw"], ones256, ya.reshape(t, ATTN_WIDTH),
                      w["wout"], w["nmx"], w["wq"], mkt, mv, w["wo"])
        x2 = _ffn(x2, w["n2"], w["wgu2"], w["wd2"], final_norm, final_norm=(l == DEPTH - 1))
    return x2.reshape(nb, seq, D_MODEL)


def _prep_weights(ffn1_norm, ffn1_w_gu, ffn1_w_down, mix_norm, w_in, four_w, dn_conv, dn_A_log, dn_dt_bias,
                  dn_out_norm, attn_q_norm, attn_k_norm, w_out, mem_norm_x, mem_norm_m, mem_wq, mem_wkv, mem_wo,
                  ffn2_norm, ffn2_w_gu, ffn2_w_down):
    lw = []
    o_ab = FOUR_WIDTH + 3 * DN_WIDTH
    o_gate = o_ab + 4 * DN_HEADS
    for l in range(DEPTH):
        wi = w_in[l]
        win = jnp.concatenate([wi[:, :o_ab], wi[:, o_gate:], wi[:, o_ab:o_gate],
                               jnp.zeros((D_MODEL, LANES - 4 * DN_HEADS), F32)], axis=1).astype(BF16)
        fw = jnp.zeros((FOUR_WIDTH, FOUR_WIDTH), F32)
        for g in range(FOUR_GROUPS):
            fw = lax.dynamic_update_slice(fw, four_w[l, g], (g * FOUR_DIM, g * FOUR_DIM))
        zeros8 = jnp.zeros((2 * DN_HEADS,), F32)
        lw.append(dict(
            n1=ffn1_norm[l].reshape(1, D_MODEL), wgu1=ffn1_w_gu[l].astype(BF16), wd1=ffn1_w_down[l].astype(BF16),
            nm=mix_norm[l].reshape(1, D_MODEL), win=win,
            qw=jnp.tile(attn_q_norm[l], LANES // HEAD_DIM).reshape(1, LANES),
            kw=jnp.tile(attn_k_norm[l], LANES // HEAD_DIM).reshape(1, LANES),
            fw=fw.astype(BF16), cw=dn_conv[l],
            na=jnp.concatenate([dn_A_log[l].reshape(-1), zeros8]).reshape(1, 4 * DN_HEADS),
            dtb=jnp.concatenate([dn_dt_bias[l].reshape(-1), zeros8]).reshape(1, 4 * DN_HEADS),
            onw=jnp.tile(dn_out_norm[l], DN_HEADS).reshape(1, DN_WIDTH),
            wout=w_out[l].astype(BF16), nmx=mem_norm_x[l].reshape(1, D_MODEL), nmm=mem_norm_m[l].reshape(1, D_MODEL),
            wq=mem_wq[l].astype(BF16), wkv=mem_wkv[l].astype(BF16), wo=mem_wo[l].astype(BF16),
            n2=ffn2_norm[l].reshape(1, D_MODEL), wgu2=ffn2_w_gu[l].astype(BF16), wd2=ffn2_w_down[l].astype(BF16),
        ))
    return lw


def kernel(x_prompt, x_sample, mem_prompt, mem_sample, ffn1_norm, ffn1_w_gu, ffn1_w_down, mix_norm, w_in, four_w, dn_conv, dn_A_log, dn_dt_bias, dn_out_norm, attn_q_norm, attn_k_norm, w_out, mem_norm_x, mem_norm_m, mem_wq, mem_wkv, mem_wo, ffn2_norm, ffn2_w_gu, ffn2_w_down, final_norm):
    lw = _prep_weights(ffn1_norm, ffn1_w_gu, ffn1_w_down, mix_norm, w_in, four_w, dn_conv, dn_A_log, dn_dt_bias,
                       dn_out_norm, attn_q_norm, attn_k_norm, w_out, mem_norm_x, mem_norm_m, mem_wq, mem_wkv, mem_wo,
                       ffn2_norm, ffn2_w_gu, ffn2_w_down)
    fn = final_norm.reshape(1, D_MODEL)
    return (_encoder(x_prompt, mem_prompt, lw, fn), _encoder(x_sample, mem_sample, lw, fn))
```
